```python
import jax, jax.numpy as jnp
from jax import lax
import numpy as np

D_MODEL = 1024
BATCH = 4
SEQ = 8192
DEPTH = 2

CONV_WIDTH = 512
CONV_KERNEL = 31
N_HEADS = 8
HEAD_K = 64
HEAD_V = 64
KEY_DIM = N_HEADS * HEAD_K
VAL_DIM = N_HEADS * HEAD_V
SHORT_CONV = 4
CHUNK = 64
N_GROUPS = 4
EXPERTS_PER_GROUP = 8
N_EXPERTS = N_GROUPS * EXPERTS_PER_GROUP
TOP_K = 2
D_EXPERT = 256
EPS = 1e-6

IN_SPLITS = (2 * CONV_WIDTH, KEY_DIM, KEY_DIM, VAL_DIM, VAL_DIM, N_HEADS, N_HEADS, D_MODEL, D_MODEL)
D_IN = 2 * CONV_WIDTH + 2 * KEY_DIM + 2 * VAL_DIM + 2 * N_HEADS + 2 * D_MODEL

kernel_name = 'hybrid_conformer_gdn_hiermoe'


def split_cols(z, sizes):
    offs = [int(o) for o in np.cumsum(sizes)[:-1]]
    return jnp.split(z, offs, axis=-1)


def rms_norm(x, gain):
    x32 = x.astype(jnp.float32)
    y = x32 * lax.rsqrt(jnp.mean(x32 * x32, axis=-1, keepdims=True) + EPS)
    return (y * gain.astype(jnp.float32)).astype(x.dtype)


def layer_norm(x, gain, bias):
    x32 = x.astype(jnp.float32)
    mu = jnp.mean(x32, axis=-1, keepdims=True)
    xc = x32 - mu
    y = xc * lax.rsqrt(jnp.mean(xc * xc, axis=-1, keepdims=True) + EPS)
    return (y * gain.astype(jnp.float32) + bias.astype(jnp.float32)).astype(x.dtype)


def l2_normalize(x):
    x32 = x.astype(jnp.float32)
    return x32 * lax.rsqrt(jnp.sum(x32 * x32, axis=-1, keepdims=True) + EPS)


def causal_depthwise_conv(x, w):
    k, c = w.shape
    return lax.conv_general_dilated(
        x, w.reshape(k, 1, c).astype(x.dtype), window_strides=(1,),
        padding=((k - 1, 0),), dimension_numbers=('NWC', 'WIO', 'NWC'),
        feature_group_count=c)


def to_chunks(t):
    b, t_len, h, d = t.shape
    return t.reshape(b, t_len // CHUNK, CHUNK, h, d).transpose(1, 0, 3, 2, 4)


def gated_delta_rule(q, k, v, beta, g):
    b, t_len, h, dk = q.shape
    dv = v.shape[-1]
    qc, kc, vc = to_chunks(q), to_chunks(k), to_chunks(v)
    bc = to_chunks(beta[..., None])[..., 0]
    gcum = jnp.cumsum(to_chunks(g[..., None])[..., 0], axis=-1)
    causal = jnp.tril(jnp.ones((CHUNK, CHUNK), dtype=bool))
    strict = jnp.tril(jnp.ones((CHUNK, CHUNK), dtype=bool), -1)
    decay = jnp.exp(jnp.where(causal, gcum[..., :, None] - gcum[..., None, :], -jnp.inf))
    kk = jnp.einsum('nbhid,nbhjd->nbhij', kc, kc)
    lower = jnp.where(strict, bc[..., :, None] * kk * decay, 0.0)
    eye = jnp.eye(CHUNK, dtype=jnp.float32)
    rhs = jnp.concatenate([vc * bc[..., None], kc * (bc * jnp.exp(gcum))[..., None]], axis=-1)
    sol = lax.linalg.triangular_solve(lower + eye, rhs, left_side=True, lower=True, unit_diagonal=True)
    u, w = sol[..., :dv], sol[..., dv:]
    qk = jnp.einsum('nbhid,nbhjd->nbhij', qc, kc) * decay
    q_dec = qc * jnp.exp(gcum)[..., None]
    k_dec = kc * jnp.exp(gcum[..., -1:] - gcum)[..., None]
    g_last = jnp.exp(gcum[..., -1])

    def step(state, xs):
        qk_i, q_i, k_i, u_i, w_i, gl = xs
        v_new = u_i - jnp.einsum('bhck,bhkv->bhcv', w_i, state)
        o = jnp.einsum('bhck,bhkv->bhcv', q_i, state) + jnp.einsum('bhij,bhjv->bhiv', qk_i, v_new)
        state = state * gl[..., None, None] + jnp.einsum('bhck,bhcv->bhkv', k_i, v_new)
        return state, o

    s0 = jnp.zeros((b, h, dk, dv), dtype=jnp.float32)
    _, o = lax.scan(step, s0, (qk, q_dec, k_dec, u, w, g_last))
    return o.transpose(1, 0, 3, 2, 4).reshape(b, t_len, h, dv)


def hybrid_mixer(h, w_in, conv_dw_w, conv_dw_b, conv_ln_gain, conv_ln_bias, w_conv_out,
                 gdn_conv_w, gdn_a_log, gdn_dt_bias, gdn_norm_gain, w_gdn_out, w_out):
    bsz, t_len, _ = h.shape
    z = h @ w_in
    c_in, q, k, v, zg, b_raw, a_raw, gate_c, gate_d = split_cols(z, IN_SPLITS)

    u = c_in[..., :CONV_WIDTH] * jax.nn.sigmoid(c_in[..., CONV_WIDTH:])
    u = causal_depthwise_conv(u, conv_dw_w) + conv_dw_b.astype(u.dtype)
    u = jax.nn.silu(layer_norm(u, conv_ln_gain, conv_ln_bias))
    y_conv = u @ w_conv_out

    qkv = jax.nn.silu(causal_depthwise_conv(jnp.concatenate([q, k, v], axis=-1), gdn_conv_w))
    q, k, v = split_cols(qkv, (KEY_DIM, KEY_DIM, VAL_DIM))
    q = l2_normalize(q.reshape(bsz, t_len, N_HEADS, HEAD_K)) * (HEAD_K ** -0.5)
    k = l2_normalize(k.reshape(bsz, t_len, N_HEADS, HEAD_K))
    v = v.reshape(bsz, t_len, N_HEADS, HEAD_V).astype(jnp.float32)
    beta = jax.nn.sigmoid(b_raw.astype(jnp.float32))
    g = -jnp.exp(gdn_a_log.astype(jnp.float32)) * jax.nn.softplus(
        a_raw.astype(jnp.float32) + gdn_dt_bias.astype(jnp.float32))
    o = gated_delta_rule(q, k, v, beta, g)
    o = rms_norm(o, gdn_norm_gain) * jax.nn.silu(zg.reshape(bsz, t_len, N_HEADS, HEAD_V).astype(jnp.float32))
    y_delta = o.reshape(bsz, t_len, VAL_DIM).astype(h.dtype) @ w_gdn_out

    y = jax.nn.sigmoid(gate_c) * y_conv + jax.nn.sigmoid(gate_d) * y_delta
    return y @ w_out


def hierarchical_moe(h, w_router_group, b_router_group, w_router_expert, b_router_expert,
                     w_expert_gate, w_expert_up, w_expert_down):
    bsz, t_len, d = h.shape
    t = h.reshape(-1, d)
    n_tok = t.shape[0]
    group_p = jax.nn.softmax((t @ w_router_group).astype(jnp.float32) + b_router_group.astype(jnp.float32), axis=-1)
    gp, gi = lax.top_k(group_p, 1)
    exp_logits = ((t @ w_router_expert).astype(jnp.float32) + b_router_expert.astype(jnp.float32)
                  ).reshape(n_tok, N_GROUPS, EXPERTS_PER_GROUP)
    sel = jnp.take_along_axis(exp_logits, gi[:, :, None], axis=1)[:, 0]
    ep = jax.nn.softmax(sel, axis=-1)
    ev, ei = lax.top_k(ep, TOP_K)
    ev = ev / jnp.sum(ev, axis=-1, keepdims=True)
    weights = gp * ev
    expert_id = gi * EXPERTS_PER_GROUP + ei
    combine = jnp.sum(jax.nn.one_hot(expert_id, N_EXPERTS, dtype=jnp.float32) * weights[..., None], axis=1)
    combine = combine.astype(t.dtype)
    out = jnp.zeros_like(t)
    for e in range(N_EXPERTS):
        hid = jax.nn.silu(t @ w_expert_gate[e]) * (t @ w_expert_up[e])
        out = out + combine[:, e:e + 1] * (hid @ w_expert_down[e])
    return out.reshape(bsz, t_len, d)


def setup_inputs(seed: int = 0) -> dict:
    key = jax.random.key(seed)
    ks = jax.random.split(key, 24)
    L = DEPTH

    def nrm(k, shape, scale):
        return jax.random.normal(k, shape, dtype=jnp.float32) * scale

    def gain(k, shape):
        return 1.0 + 0.01 * jax.random.normal(k, shape, dtype=jnp.float32)

    dt = jnp.exp(jax.random.uniform(ks[12], (L, N_HEADS), minval=np.log(1e-3), maxval=np.log(1e-1)))
    return {
        'x': nrm(ks[0], (BATCH, SEQ, D_MODEL), 1.0),
        'norm_mix_gain': gain(ks[1], (L, D_MODEL)),
        'w_in': nrm(ks[2], (L, D_MODEL, D_IN), D_MODEL ** -0.5),
        'conv_dw_w': nrm(ks[3], (L, CONV_KERNEL, CONV_WIDTH), CONV_KERNEL ** -0.5),
        'conv_dw_b': nrm(ks[4], (L, CONV_WIDTH), 0.01),
        'conv_ln_gain': gain(ks[5], (L, CONV_WIDTH)),
        'conv_ln_bias': nrm(ks[6], (L, CONV_WIDTH), 0.01),
        'w_conv_out': nrm(ks[7], (L, CONV_WIDTH, D_MODEL), CONV_WIDTH ** -0.5),
        'gdn_conv_w': nrm(ks[8], (L, SHORT_CONV, 2 * KEY_DIM + VAL_DIM), SHORT_CONV ** -0.5),
        'gdn_a_log': jnp.log(jax.random.uniform(ks[9], (L, N_HEADS), minval=1.0, maxval=16.0)),
        'gdn_dt_bias': jnp.log(jnp.expm1(dt)),
        'gdn_norm_gain': gain(ks[10], (L, HEAD_V)),
        'w_gdn_out': nrm(ks[11], (L, VAL_DIM, D_MODEL), VAL_DIM ** -0.5),
        'w_out': nrm(ks[13], (L, D_MODEL, D_MODEL), D_MODEL ** -0.5),
        'norm_ffn_gain': gain(ks[14], (L, D_MODEL)),
        'w_router_group': nrm(ks[15], (L, D_MODEL, N_GROUPS), D_MODEL ** -0.5),
        'b_router_group': nrm(ks[16], (L, N_GROUPS), 0.01),
        'w_router_expert': nrm(ks[17], (L, D_MODEL, N_EXPERTS), D_MODEL ** -0.5),
        'b_router_expert': nrm(ks[18], (L, N_EXPERTS), 0.01),
        'w_expert_gate': nrm(ks[19], (L, N_EXPERTS, D_MODEL, D_EXPERT), D_MODEL ** -0.5),
        'w_expert_up': nrm(ks[20], (L, N_EXPERTS, D_MODEL, D_EXPERT), D_MODEL ** -0.5),
        'w_expert_down': nrm(ks[21], (L, N_EXPERTS, D_EXPERT, D_MODEL), D_EXPERT ** -0.5),
        'final_norm_gain': gain(ks[22], (D_MODEL,)),
    }


def reference(x, norm_mix_gain, w_in, conv_dw_w, conv_dw_b, conv_ln_gain, conv_ln_bias, w_conv_out,
              gdn_conv_w, gdn_a_log, gdn_dt_bias, gdn_norm_gain, w_gdn_out, w_out,
              norm_ffn_gain, w_router_group, b_router_group, w_router_expert, b_router_expert,
              w_expert_gate, w_expert_up, w_expert_down, final_norm_gain):
    for l in range(DEPTH):
        h = rms_norm(x, norm_mix_gain[l])
        x = x + hybrid_mixer(h, w_in[l], conv_dw_w[l], conv_dw_b[l], conv_ln_gain[l], conv_ln_bias[l],
                             w_conv_out[l], gdn_conv_w[l], gdn_a_log[l], gdn_dt_bias[l], gdn_norm_gain[l],
                             w_gdn_out[l], w_out[l])
        h = rms_norm(x, norm_ffn_gain[l])
        x = x + hierarchical_moe(h, w_router_group[l], b_router_group[l], w_router_expert[l],
                                 b_router_expert[l], w_expert_gate[l], w_expert_up[l], w_expert_down[l])
    return rms_norm(x, final_norm_gain)
```

```python
import functools

import jax
import jax.numpy as jnp
from jax import lax
from jax.experimental import pallas as pl
from jax.experimental.pallas import tpu as pltpu

F32 = jnp.float32
BF16 = jnp.bfloat16
I32 = jnp.int32

D_MODEL = 1024
CONV_WIDTH = 512
CONV_KERNEL = 31
N_HEADS = 8
HEAD_K = 64
HEAD_V = 64
KEY_DIM = N_HEADS * HEAD_K
VAL_DIM = N_HEADS * HEAD_V
SHORT_CONV = 4
CHUNK = 64
N_GROUPS = 4
EXPERTS_PER_GROUP = 8
N_EXPERTS = N_GROUPS * EXPERTS_PER_GROUP
D_EXPERT = 256
EPS = 1e-6

SUBLANES = 8
LANES = 128
VMEM_LIMIT_BYTES = 56 * 1024 * 1024

CONV_HALO = 32
QKV_HALO = SUBLANES
HALF = 4 * HEAD_K
ROW_TILE = 256
RANK_BITS = 17
ROUTER_ROWS = 40

C_CONV = 0
C_QKV = C_CONV + 2 * CONV_WIDTH
C_ZG = C_QKV + 2 * KEY_DIM + VAL_DIM
C_BETA = C_ZG + VAL_DIM
C_DECAY = C_BETA + KEY_DIM
C_GATE_C = C_DECAY + KEY_DIM
C_GATE_D = C_GATE_C + D_MODEL
D_IN_WIDE = C_GATE_D + D_MODEL


def _dot(a, b):
    return jnp.dot(a, b, preferred_element_type=F32)


def _dot_nt(a, b):
    return lax.dot_general(a, b, (((1,), (1,)), ((), ())), preferred_element_type=F32)


def _dot_tn(a, b):
    return lax.dot_general(a, b, (((0,), (0,)), ((), ())), preferred_element_type=F32)


def _sigmoid(x):
    return 1.0 / (1.0 + jnp.exp(-x))


def _silu(x):
    return x * _sigmoid(x)


def _rms_rows(x, gain):
    return x * lax.rsqrt(jnp.mean(x * x, axis=-1, keepdims=True) + EPS) * gain


def _const_spec(shape):
    zeros = (0,) * len(shape)
    return pl.BlockSpec(shape, lambda *_: zeros, pipeline_mode=pl.Buffered(1))


def _inproj_kernel(x_ref, gain_ref, w_ref, dww_ref, dwb_ref, lng_ref, lnb_ref, wco_ref,
                   scw_ref, alog_ref, dtb_ref, ones_ref,
                   yc_ref, q_ref, k_ref, v_ref, beta_ref, g_ref, szg_ref, sd_ref,
                   ubuf, qbuf):
    tm = x_ref.shape[1]

    @pl.when(pl.program_id(1) == 0)
    def _():
        ubuf[0:CONV_HALO, :] = jnp.zeros((CONV_HALO, CONV_WIDTH), F32)
        qbuf[0:QKV_HALO, :] = jnp.zeros((QKV_HALO, 2 * KEY_DIM + VAL_DIM), F32)

    h = _rms_rows(x_ref[0], gain_ref[...]).astype(BF16)

    c = _dot(h, w_ref[:, C_CONV:C_QKV])
    ubuf[CONV_HALO:CONV_HALO + tm, :] = c[:, :CONV_WIDTH] * _sigmoid(c[:, CONV_WIDTH:])
    acc = jnp.zeros((tm, CONV_WIDTH), F32) + dwb_ref[...]
    first = CONV_HALO - (CONV_KERNEL - 1)
    for r in range(SUBLANES):
        taps = [j for j in range(CONV_KERNEL) if (first + j) % SUBLANES == r]
        if not taps:
            continue
        lo = first + taps[0]
        span = taps[-1] - taps[0]
        xr = ubuf[lo:lo + span + tm, :]
        for j in taps:
            off = j - taps[0]
            acc = acc + xr[off:off + tm, :] * dww_ref[j:j + 1, :]
    ubuf[0:CONV_HALO, :] = ubuf[tm:tm + CONV_HALO, :]
    mu = jnp.mean(acc, axis=-1, keepdims=True)
    xc = acc - mu
    ln = xc * lax.rsqrt(jnp.mean(xc * xc, axis=-1, keepdims=True) + EPS) * lng_ref[...] + lnb_ref[...]
    y_conv = _dot(_silu(ln).astype(BF16), wco_ref[...])
    gate_c = _sigmoid(_dot(h, w_ref[:, C_GATE_C:C_GATE_D]))
    yc_ref[0] = (gate_c * y_conv).astype(yc_ref.dtype)

    qbuf[QKV_HALO:QKV_HALO + tm, :] = _dot(h, w_ref[:, C_QKV:C_ZG])
    first = QKV_HALO - (SHORT_CONV - 1)
    qkv = jnp.zeros((tm, 2 * KEY_DIM + VAL_DIM), F32)
    for j in range(SHORT_CONV):
        qkv = qkv + qbuf[first + j:first + j + tm, :] * scw_ref[j:j + 1, :]
    qbuf[0:QKV_HALO, :] = qbuf[tm:tm + QKV_HALO, :]
    qkv = _silu(qkv)
    q = qkv[:, :KEY_DIM]
    k = qkv[:, KEY_DIM:2 * KEY_DIM]
    ones = ones_ref[...]
    q_ref[0] = q * lax.rsqrt(_dot((q * q).astype(BF16), ones) + EPS) * (HEAD_K ** -0.5)
    k_ref[0] = k * lax.rsqrt(_dot((k * k).astype(BF16), ones) + EPS)
    v_ref[0] = qkv[:, 2 * KEY_DIM:]

    szg_ref[0] = _silu(_dot(h, w_ref[:, C_ZG:C_BETA])).astype(szg_ref.dtype)
    beta_ref[0] = _sigmoid(_dot(h, w_ref[:, C_BETA:C_DECAY]))
    a = _dot(h, w_ref[:, C_DECAY:C_GATE_C]) + dtb_ref[...]
    softplus = jnp.maximum(a, 0.0) + jnp.log1p(jnp.exp(-jnp.abs(a)))
    g_ref[0] = -jnp.exp(alog_ref[...]) * softplus
    sd_ref[0] = _sigmoid(_dot(h, w_ref[:, C_GATE_D:D_IN_WIDE])).astype(sd_ref.dtype)


def _inproj(x, gain, w_wide, dww, dwb, lng, lnb, wco, scw, alog_x, dtb_x, ones_bd, tm):
    b, t, d = x.shape
    grid = (b, t // tm)
    tile = lambda width: pl.BlockSpec((1, tm, width), lambda bi, ti: (bi, ti, 0))
    out_shape = (
        jax.ShapeDtypeStruct((b, t, D_MODEL), BF16),
        jax.ShapeDtypeStruct((b, t, KEY_DIM), F32),
        jax.ShapeDtypeStruct((b, t, KEY_DIM), F32),
        jax.ShapeDtypeStruct((b, t, VAL_DIM), F32),
        jax.ShapeDtypeStruct((b, t, KEY_DIM), F32),
        jax.ShapeDtypeStruct((b, t, KEY_DIM), F32),
        jax.ShapeDtypeStruct((b, t, VAL_DIM), BF16),
        jax.ShapeDtypeStruct((b, t, D_MODEL), BF16),
    )
    return pl.pallas_call(
        _inproj_kernel,
        grid=grid,
        in_specs=[
            tile(d),
            _const_spec((1, d)),
            _const_spec((d, D_IN_WIDE)),
            _const_spec((CONV_KERNEL, CONV_WIDTH)),
            _const_spec((1, CONV_WIDTH)),
            _const_spec((1, CONV_WIDTH)),
            _const_spec((1, CONV_WIDTH)),
            _const_spec((CONV_WIDTH, D_MODEL)),
            _const_spec((SHORT_CONV, 2 * KEY_DIM + VAL_DIM)),
            _const_spec((1, KEY_DIM)),
            _const_spec((1, KEY_DIM)),
            _const_spec((KEY_DIM, KEY_DIM)),
        ],
        out_specs=(tile(D_MODEL), tile(KEY_DIM), tile(KEY_DIM), tile(VAL_DIM), tile(KEY_DIM),
                   tile(KEY_DIM), tile(VAL_DIM), tile(D_MODEL)),
        out_shape=out_shape,
        scratch_shapes=[
            pltpu.VMEM((CONV_HALO + tm, CONV_WIDTH), F32),
            pltpu.VMEM((QKV_HALO + tm, 2 * KEY_DIM + VAL_DIM), F32),
        ],
        compiler_params=pltpu.CompilerParams(
            dimension_semantics=("arbitrary", "arbitrary"), vmem_limit_bytes=VMEM_LIMIT_BYTES),
        name="inproj",
    )(x, gain, w_wide, dww, dwb, lng, lnb, wco, scw, alog_x, dtb_x, ones_bd)


def _gdn_chunk(q, k, v, bx, gx, s, tri, diag, causal, strict, bdmask):
    def bd(m):
        mb = m.astype(BF16)
        return jnp.where(bdmask, jnp.concatenate([mb, mb, mb, mb], axis=0), jnp.zeros((), BF16))

    g1 = gx.astype(BF16)
    r1 = gx - g1.astype(F32)
    g2 = r1.astype(BF16)
    g3 = (r1 - g2.astype(F32)).astype(BF16)
    gc3 = _dot(tri, jnp.concatenate([g1, g2, g3], axis=1))
    gc = gc3[:, :HALF] + gc3[:, HALF:2 * HALF] + gc3[:, 2 * HALF:]
    gl = gc[CHUNK - 1:CHUNK, :]
    eg = jnp.exp(gc)
    kb = k * bx
    w_in = kb * eg
    vb = v * bx
    qd = q * eg
    kd = k * jnp.exp(gl - gc)
    gc_t = jnp.sum(jnp.where(diag, gc, 0.0), axis=0, keepdims=True)
    decay = jnp.where(causal, jnp.exp(jnp.minimum(gc - gc_t, 0.0)), 0.0)

    kq = jnp.concatenate([k, q], axis=0).astype(BF16)
    s1 = _dot_nt(kq, bd(k))
    kk = s1[:CHUNK]
    qkd = s1[CHUNK:] * decay
    p0 = jnp.where(strict, -(kk * bx * decay), 0.0)

    t = jnp.where(diag, 1.0, 0.0) + p0
    qp = _dot(p0.astype(BF16), bd(p0))
    for lvl in range(1, 6):
        qb = qp.astype(BF16)
        if lvl < 5:
            both = _dot(qb, jnp.concatenate([bd(qp), bd(t)], axis=1))
            qp = both[:, :HALF]
            t = t + both[:, HALF:]
        else:
            t = t + _dot(qb, bd(t))
    uw = _dot(t.astype(BF16), jnp.concatenate([bd(vb), bd(w_in)], axis=1))
    u = uw[:, :HALF]
    w = uw[:, HALF:]

    pm = _dot(jnp.concatenate([w, qd], axis=0).astype(BF16), s.astype(BF16))
    v_new = u - pm[:CHUNK]
    o = pm[CHUNK:] + _dot(qkd.astype(BF16), bd(v_new))
    s_new = s * jnp.exp(gl) + jnp.where(bdmask, _dot_tn(kd.astype(BF16), v_new.astype(BF16)), 0.0)
    return o, s_new


def _gdn_kernel(q_ref, k_ref, v_ref, b_ref, g_ref, o_ref, s_ref):
    tc = q_ref.shape[1]

    @pl.when(pl.program_id(2) == 0)
    def _():
        s_ref[...] = jnp.zeros((HALF, HALF), F32)

    row = lax.broadcasted_iota(I32, (CHUNK, HALF), 0)
    col = lax.broadcasted_iota(I32, (CHUNK, HALF), 1) & (CHUNK - 1)
    diag = row == col
    causal = row >= col
    strict = row > col
    tri = (lax.broadcasted_iota(I32, (CHUNK, CHUNK), 0)
           >= lax.broadcasted_iota(I32, (CHUNK, CHUNK), 1)).astype(BF16)
    bdmask = ((lax.broadcasted_iota(I32, (HALF, HALF), 0) >> 6)
              == (lax.broadcasted_iota(I32, (HALF, HALF), 1) >> 6))

    s = s_ref[...]
    for c in range(tc // CHUNK):
        sl = slice(c * CHUNK, (c + 1) * CHUNK)
        o, s = _gdn_chunk(q_ref[0, sl, :], k_ref[0, sl, :], v_ref[0, sl, :], b_ref[0, sl, :],
                          g_ref[0, sl, :], s, tri, diag, causal, strict, bdmask)
        o_ref[0, sl, :] = o
    s_ref[...] = s


def _gdn(q, k, v, beta_x, g_x, tc):
    b, t, _ = q.shape
    grid = (b, KEY_DIM // HALF, t // tc)
    spec = pl.BlockSpec((1, tc, HALF), lambda bi, hi, ti: (bi, ti, hi))
    return pl.pallas_call(
        _gdn_kernel,
        grid=grid,
        in_specs=[spec] * 5,
        out_specs=spec,
        out_shape=jax.ShapeDtypeStruct((b, t, VAL_DIM), F32),
        scratch_shapes=[pltpu.VMEM((HALF, HALF), F32)],
        compiler_params=pltpu.CompilerParams(
            dimension_semantics=("arbitrary", "arbitrary", "arbitrary"),
            vmem_limit_bytes=VMEM_LIMIT_BYTES),
        name="gdn",
    )(q, k, v, beta_x, g_x)


def _merge_kernel(x_ref, yc_ref, o_ref, szg_ref, sd_ref, ng_ref, wgo_ref, wo_ref, fg_ref,
                  wr_ref, br_ref, ones_ref, eye_ref,
                  xo_ref, eid_ref, wts_ref):
    tm = x_ref.shape[0]
    o = o_ref[...]
    ms = _dot((o * o).astype(BF16), ones_ref[...]) * (1.0 / HEAD_V)
    on = o * lax.rsqrt(ms + EPS) * ng_ref[...] * szg_ref[...].astype(F32)
    y_delta = _dot(on.astype(BF16), wgo_ref[...])
    y = yc_ref[...].astype(F32) + sd_ref[...].astype(F32) * y_delta
    x_new = x_ref[...] + _dot(y.astype(BF16), wo_ref[...])
    xo_ref[...] = x_new

    h2 = _rms_rows(x_new, fg_ref[...]).astype(BF16)
    logits = _dot_nt(wr_ref[...], h2) + br_ref[...]
    gl = logits[N_EXPERTS:N_EXPERTS + SUBLANES]
    grow = lax.broadcasted_iota(I32, (SUBLANES, tm), 0)
    gl = jnp.where(grow < N_GROUPS, gl, -jnp.inf)
    gmax = jnp.max(gl, axis=0, keepdims=True)
    group_p = 1.0 / jnp.sum(jnp.exp(gl - gmax), axis=0, keepdims=True)
    gi = jnp.min(jnp.where(gl == gmax, grow, SUBLANES), axis=0, keepdims=True)
    sel = jnp.zeros((EXPERTS_PER_GROUP, tm), F32)
    for g in range(N_GROUPS):
        sel = jnp.where(gi == g, logits[g * EXPERTS_PER_GROUP:(g + 1) * EXPERTS_PER_GROUP], sel)
    erow = lax.broadcasted_iota(I32, (EXPERTS_PER_GROUP, tm), 0)
    m1 = jnp.max(sel, axis=0, keepdims=True)
    i1 = jnp.min(jnp.where(sel == m1, erow, EXPERTS_PER_GROUP), axis=0, keepdims=True)
    sel2 = jnp.where(erow == i1, -jnp.inf, sel)
    m2 = jnp.max(sel2, axis=0, keepdims=True)
    i2 = jnp.min(jnp.where(sel2 == m2, erow, EXPERTS_PER_GROUP), axis=0, keepdims=True)
    e21 = jnp.exp(m2 - m1)
    w1 = group_p / (1.0 + e21)
    w2 = w1 * e21
    eid_ref[...] = jnp.concatenate(
        [gi * EXPERTS_PER_GROUP + i1, gi * EXPERTS_PER_GROUP + i2], axis=0)

    wrow = jnp.concatenate([w1, w2], axis=0)
    p1 = wrow.astype(BF16)
    r1 = wrow - p1.astype(F32)
    p2 = r1.astype(BF16)
    p3 = (r1 - p2.astype(F32)).astype(BF16)
    parts = jnp.concatenate([p1, p2, p3, jnp.zeros((2, tm), BF16)], axis=0)
    cols = _dot_nt(eye_ref[...], parts)
    wcol = cols[:, 0:2] + cols[:, 2:4] + cols[:, 4:6]
    wts_ref[...] = jnp.concatenate([wcol, jnp.zeros((tm, SUBLANES - 2), F32)], axis=1)


def _merge(x2, yc, o, szg, sd, ng_x, wgo, wo, fg, wr_t, br_t, ones_bd, eye, tm):
    n, d = x2.shape
    grid = (n // tm,)
    tile = lambda width: pl.BlockSpec((tm, width), lambda i: (i, 0))
    return pl.pallas_call(
        _merge_kernel,
        grid=grid,
        in_specs=[
            tile(d), tile(d), tile(VAL_DIM), tile(VAL_DIM), tile(d),
            _const_spec((1, VAL_DIM)),
            _const_spec((VAL_DIM, d)),
            _const_spec((d, d)),
            _const_spec((1, d)),
            _const_spec((ROUTER_ROWS, d)),
            _const_spec((ROUTER_ROWS, 1)),
            _const_spec((VAL_DIM, VAL_DIM)),
            _const_spec((tm, tm)),
        ],
        out_specs=(tile(d),
                   pl.BlockSpec((2, tm), lambda i: (0, i)),
                   pl.BlockSpec((tm, SUBLANES), lambda i: (i, 0))),
        out_shape=(jax.ShapeDtypeStruct((n, d), F32),
                   jax.ShapeDtypeStruct((2, n), I32),
                   jax.ShapeDtypeStruct((n, SUBLANES), F32)),
        compiler_params=pltpu.CompilerParams(
            dimension_semantics=("arbitrary",), vmem_limit_bytes=VMEM_LIMIT_BYTES),
        name="merge",
    )(x2, yc, o, szg, sd, ng_x, wgo, wo, fg, wr_t, br_t, ones_bd, eye)


def _plan_kernel(eid_ref, su_ref, code_ref, cnt_ref, carry):
    tp = eid_ref.shape[1]

    @pl.when(pl.program_id(0) == 0)
    def _():
        carry[...] = jnp.zeros((N_EXPERTS, LANES), F32)

    e = eid_ref[...]
    rows = lax.broadcasted_iota(I32, (N_EXPERTS, tp), 0)
    hit0 = rows == e[0:1]
    hit1 = rows == e[1:2]
    hits = jnp.where(hit0 | hit1, 1.0, 0.0)
    before = _dot(hits.astype(BF16), su_ref[...]) + carry[:, 0:1]
    r0 = jnp.sum(jnp.where(hit0, before, 0.0), axis=0, keepdims=True)
    r1 = jnp.sum(jnp.where(hit1, before, 0.0), axis=0, keepdims=True)
    rank = jnp.concatenate([r0, r1], axis=0).astype(I32)
    code_ref[...] = (e << RANK_BITS) | rank
    carry[...] = carry[...] + jnp.sum(hits, axis=1, keepdims=True)
    cnt_ref[...] = carry[...]


def _plan(eid, su, tp):
    n = eid.shape[1]
    return pl.pallas_call(
        _plan_kernel,
        grid=(n // tp,),
        in_specs=[pl.BlockSpec((2, tp), lambda i: (0, i)), _const_spec((tp, tp))],
        out_specs=(pl.BlockSpec((2, tp), lambda i: (0, i)),
                   pl.BlockSpec((N_EXPERTS, LANES), lambda i: (0, 0))),
        out_shape=(jax.ShapeDtypeStruct((2, n), I32),
                   jax.ShapeDtypeStruct((N_EXPERTS, LANES), F32)),
        scratch_shapes=[pltpu.VMEM((N_EXPERTS, LANES), F32)],
        compiler_params=pltpu.CompilerParams(dimension_semantics=("arbitrary",)),
        name="plan",
    )(eid, su)


def _sorted_row(code_ref, offs_ref, idx):
    c = code_ref[idx]
    return offs_ref[c >> RANK_BITS] + (c & ((1 << RANK_BITS) - 1))


def _dispatch_kernel(code_ref, offs_ref, x_hbm, xs_in, xs_hbm, sem, *, n_tokens, td):
    del xs_in
    base = pl.program_id(0) * td

    def row_copy(t, slot):
        p = _sorted_row(code_ref, offs_ref, slot * n_tokens + t)
        return pltpu.make_async_copy(x_hbm.at[pl.ds(t, 1)], xs_hbm.at[pl.ds(p, 1)], sem)

    def issue(i, carry):
        row_copy(base + i, 0).start()
        row_copy(base + i, 1).start()
        return carry

    lax.fori_loop(0, td, issue, 0)

    def drain(i, carry):
        row_copy(base + i, 0).wait()
        row_copy(base + i, 1).wait()
        return carry

    lax.fori_loop(0, td, drain, 0)


def _dispatch(code_flat, offs, x2, n_rows, td):
    n, d = x2.shape
    xs0 = jnp.zeros((n_rows, d), F32)
    grid_spec = pltpu.PrefetchScalarGridSpec(
        num_scalar_prefetch=2,
        grid=(n // td,),
        in_specs=[pl.BlockSpec(memory_space=pl.ANY), pl.BlockSpec(memory_space=pl.ANY)],
        out_specs=pl.BlockSpec(memory_space=pl.ANY),
        scratch_shapes=[pltpu.SemaphoreType.DMA],
    )
    return pl.pallas_call(
        functools.partial(_dispatch_kernel, n_tokens=n, td=td),
        grid_spec=grid_spec,
        out_shape=jax.ShapeDtypeStruct((n_rows, d), F32),
        input_output_aliases={3: 0},
        compiler_params=pltpu.CompilerParams(dimension_semantics=("arbitrary",)),
        name="dispatch",
    )(code_flat, offs, x2, xs0)


def _expert_kernel(te_ref, xs_ref, fg_ref, wg_ref, wu_ref, wd_ref, ys_ref):
    live = te_ref[pl.program_id(0)] < N_EXPERTS

    @pl.when(live)
    def _():
        h = _rms_rows(xs_ref[...], fg_ref[...]).astype(BF16)
        hid = _silu(_dot(h, wg_ref[0])) * _dot(h, wu_ref[0])
        ys_ref[...] = _dot(hid.astype(BF16), wd_ref[0])

    @pl.when(jnp.logical_not(live))
    def _():
        ys_ref[...] = jnp.zeros(ys_ref.shape, F32)


def _experts(tile_expert, xs, fg, wg, wu, wd):
    n_rows, d = xs.shape
    expert = lambda i, te: jnp.minimum(te[i], N_EXPERTS - 1)
    grid_spec = pltpu.PrefetchScalarGridSpec(
        num_scalar_prefetch=1,
        grid=(n_rows // ROW_TILE,),
        in_specs=[
            pl.BlockSpec((ROW_TILE, d), lambda i, te: (i, 0)),
            pl.BlockSpec((1, d), lambda i, te: (0, 0)),
            pl.BlockSpec((1, d, D_EXPERT), lambda i, te: (expert(i, te), 0, 0)),
            pl.BlockSpec((1, d, D_EXPERT), lambda i, te: (expert(i, te), 0, 0)),
            pl.BlockSpec((1, D_EXPERT, d), lambda i, te: (expert(i, te), 0, 0)),
        ],
        out_specs=pl.BlockSpec((ROW_TILE, d), lambda i, te: (i, 0)),
    )
    return pl.pallas_call(
        _expert_kernel,
        grid_spec=grid_spec,
        out_shape=jax.ShapeDtypeStruct((n_rows, d), F32),
        compiler_params=pltpu.CompilerParams(
            dimension_semantics=("arbitrary",), vmem_limit_bytes=VMEM_LIMIT_BYTES),
        name="experts",
    )(tile_expert, xs, fg, wg, wu, wd)


def _combine_kernel(code_ref, offs_ref, x_ref, w_ref, fin_ref, ys_hbm, out_ref, ybuf, sem,
                    *, n_tokens, tc, final_norm):
    base = pl.program_id(0) * tc

    def row_copy(i, slot):
        p = _sorted_row(code_ref, offs_ref, slot * n_tokens + base + i)
        return pltpu.make_async_copy(ys_hbm.at[pl.ds(p, 1)], ybuf.at[slot, pl.ds(i, 1)], sem)

    def issue(i, carry):
        row_copy(i, 0).start()
        row_copy(i, 1).start()
        return carry

    lax.fori_loop(0, tc, issue, 0)

    def drain(i, carry):
        row_copy(i, 0).wait()
        row_copy(i, 1).wait()
        return carry

    lax.fori_loop(0, tc, drain, 0)

    w = w_ref[...]
    out = x_ref[...] + w[:, 0:1] * ybuf[0] + w[:, 1:2] * ybuf[1]
    if final_norm:
        out = _rms_rows(out, fin_ref[...])
    out_ref[...] = out


def _combine(code_flat, offs, x2, wts, fin_gain, ys, tc, final_norm):
    n, d = x2.shape
    grid_spec = pltpu.PrefetchScalarGridSpec(
        num_scalar_prefetch=2,
        grid=(n // tc,),
        in_specs=[
            pl.BlockSpec((tc, d), lambda i, *_: (i, 0)),
            pl.BlockSpec((tc, SUBLANES), lambda i, *_: (i, 0)),
            pl.BlockSpec((1, d), lambda i, *_: (0, 0)),
            pl.BlockSpec(memory_space=pl.ANY),
        ],
        out_specs=pl.BlockSpec((tc, d), lambda i, *_: (i, 0)),
        scratch_shapes=[pltpu.VMEM((2, tc, d), F32), pltpu.SemaphoreType.DMA],
    )
    return pl.pallas_call(
        functools.partial(_combine_kernel, n_tokens=n, tc=tc, final_norm=final_norm),
        grid_spec=grid_spec,
        out_shape=jax.ShapeDtypeStruct((n, d), F32),
        compiler_params=pltpu.CompilerParams(dimension_semantics=("arbitrary",)),
        name="combine",
    )(code_flat, offs, x2, wts, fin_gain, ys)


def _tile(n, target):
    t = min(n, target)
    while n % t:
        t //= 2
    return t


def _widen_w_in(w):
    o = 0
    conv = w[:, o:o + 2 * CONV_WIDTH]; o += 2 * CONV_WIDTH
    qkv = w[:, o:o + 2 * KEY_DIM + VAL_DIM]; o += 2 * KEY_DIM + VAL_DIM
    zg = w[:, o:o + VAL_DIM]; o += VAL_DIM
    wb = w[:, o:o + N_HEADS]; o += N_HEADS
    wa = w[:, o:o + N_HEADS]; o += N_HEADS
    gc = w[:, o:o + D_MODEL]; o += D_MODEL
    gd = w[:, o:o + D_MODEL]
    return jnp.concatenate(
        [conv, qkv, zg, jnp.repeat(wb, HEAD_K, axis=1), jnp.repeat(wa, HEAD_K, axis=1), gc, gd],
        axis=1).astype(BF16)


def kernel(x, norm_mix_gain, w_in, conv_dw_w, conv_dw_b, conv_ln_gain, conv_ln_bias, w_conv_out,
           gdn_conv_w, gdn_a_log, gdn_dt_bias, gdn_norm_gain, w_gdn_out, w_out,
           norm_ffn_gain, w_router_group, b_router_group, w_router_expert, b_router_expert,
           w_expert_gate, w_expert_up, w_expert_down, final_norm_gain):
    b, t, d = x.shape
    n = b * t
    depth = w_in.shape[0]
    tm = _tile(t, 512)
    tc_gdn = _tile(t, 256)
    tm_merge = _tile(n, 512)
    tp = _tile(n, 512)
    td = _tile(n, 512)
    tc_comb = _tile(n, 256)
    n_rows = -(-(2 * n + N_EXPERTS * (ROW_TILE - 1)) // ROW_TILE) * ROW_TILE
    n_tiles = n_rows // ROW_TILE

    head_of_lane = jnp.arange(KEY_DIM) // HEAD_K
    ones_bd = (head_of_lane[:, None] == head_of_lane[None, :]).astype(BF16)
    eye = jnp.eye(tm_merge, dtype=BF16)
    su = (jnp.arange(tp)[:, None] < jnp.arange(tp)[None, :]).astype(BF16)
    row = lambda a: a.reshape(1, -1).astype(F32)

    for l in range(depth):
        yc, q, k, v, beta_x, g_x, szg, sd = _inproj(
            x, row(norm_mix_gain[l]), _widen_w_in(w_in[l]), conv_dw_w[l], row(conv_dw_b[l]),
            row(conv_ln_gain[l]), row(conv_ln_bias[l]), w_conv_out[l].astype(BF16), gdn_conv_w[l],
            row(jnp.repeat(gdn_a_log[l], HEAD_K)), row(jnp.repeat(gdn_dt_bias[l], HEAD_K)),
            ones_bd, tm)
        o = _gdn(q, k, v, beta_x, g_x, tc_gdn)

        wr_t = jnp.concatenate(
            [w_router_expert[l].T, w_router_group[l].T,
             jnp.zeros((ROUTER_ROWS - N_EXPERTS - N_GROUPS, d), F32)], axis=0).astype(BF16)
        br_t = jnp.concatenate(
            [b_router_expert[l], b_router_group[l],
             jnp.zeros((ROUTER_ROWS - N_EXPERTS - N_GROUPS,), F32)]).reshape(ROUTER_ROWS, 1)
        x_mid, eid, wts = _merge(
            x.reshape(n, d), yc.reshape(n, d), o.reshape(n, VAL_DIM), szg.reshape(n, VAL_DIM),
            sd.reshape(n, d), row(jnp.tile(gdn_norm_gain[l], N_HEADS)), w_gdn_out[l].astype(BF16),
            w_out[l].astype(BF16), row(norm_ffn_gain[l]), wr_t, br_t, ones_bd, eye, tm_merge)

        code, counts = _plan(eid, su, tp)
        cnt = counts[:, 0].astype(I32)
        padded = ((cnt + ROW_TILE - 1) // ROW_TILE) * ROW_TILE
        ends = jnp.cumsum(padded)
        offs = ends - padded
        tile_expert = jnp.sum(
            (jnp.arange(n_tiles, dtype=I32)[:, None] * ROW_TILE >= ends[None, :]).astype(I32), axis=1)
        code_flat = code.reshape(2 * n)

        xs = _dispatch(code_flat, offs, x_mid, n_rows, td)
        ys = _experts(tile_expert, xs, row(norm_ffn_gain[l]), w_expert_gate[l].astype(BF16),
                      w_expert_up[l].astype(BF16), w_expert_down[l].astype(BF16))
        last = l == depth - 1
        x = _combine(code_flat, offs, x_mid, wts, row(final_norm_gain), ys, tc_comb, last
                     ).reshape(b, t, d)
    return x
```

```python
import functools

import jax
import jax.numpy as jnp
from jax import lax
from jax.experimental import pallas as pl
from jax.experimental.pallas import tpu as pltpu

F32 = jnp.float32
BF16 = jnp.bfloat16
I32 = jnp.int32

D_MODEL = 1024
CONV_WIDTH = 512
CONV_KERNEL = 31
N_HEADS = 8
HEAD_K = 64
HEAD_V = 64
KEY_DIM = N_HEADS * HEAD_K
VAL_DIM = N_HEADS * HEAD_V
QKV_DIM = 2 * KEY_DIM + VAL_DIM
SHORT_CONV = 4
CHUNK = 64
N_GROUPS = 4
EXPERTS_PER_GROUP = 8
N_EXPERTS = N_GROUPS * EXPERTS_PER_GROUP
D_EXPERT = 256
EPS = 1e-6

SUBLANES = 8
LANES = 128
VMEM_LIMIT_BYTES = 56 * 1024 * 1024

CONV_HALO = 32
QKV_HALO = SUBLANES
HALF = 4 * HEAD_K
ROW_TILE = 256
ROUTER_ROWS = 40
SPLIT = 16

C_CONV = 0
C_QKV = C_CONV + 2 * CONV_WIDTH
C_ZG = C_QKV + QKV_DIM
C_BETA = C_ZG + VAL_DIM
C_DECAY = C_BETA + KEY_DIM
C_GATE_C = C_DECAY + KEY_DIM
C_GATE_D = C_GATE_C + D_MODEL
D_IN_WIDE = C_GATE_D + D_MODEL


def _dot(a, b):
    return jnp.dot(a, b, preferred_element_type=F32)


def _dot_nt(a, b):
    return lax.dot_general(a, b, (((1,), (1,)), ((), ())), preferred_element_type=F32)


def _dot_tn(a, b):
    return lax.dot_general(a, b, (((0,), (0,)), ((), ())), preferred_element_type=F32)


def _sigmoid(x):
    return 0.5 * jnp.tanh(0.5 * x) + 0.5


def _silu(x):
    return x * _sigmoid(x)


def _rms_rows(x, gain):
    return x * lax.rsqrt(jnp.mean(x * x, axis=-1, keepdims=True) + EPS) * gain


def _const_spec(shape):
    zeros = (0,) * len(shape)
    return pl.BlockSpec(shape, lambda *_: zeros, pipeline_mode=pl.Buffered(1))


def _shift_up(a, r):
    return a if r == 0 else pltpu.roll(a, a.shape[0] - r, 0)


def _inproj_kernel(x_ref, gain_ref, w_ref, dww_ref, dwb_ref, lng_ref, lnb_ref, wco_ref,
                   scw_ref, alog_ref, dtb_ref, ones_ref,
                   yc_ref, q_ref, k_ref, v_ref, beta_ref, g_ref, szg_ref, sd_ref,
                   ubuf, qbuf):
    tm = x_ref.shape[1]
    ext = tm + SUBLANES

    @pl.when(pl.program_id(1) == 0)
    def _():
        ubuf[0:CONV_HALO, :] = jnp.zeros((CONV_HALO, CONV_WIDTH), F32)
        ubuf[CONV_HALO + tm:, :] = jnp.zeros((SUBLANES, CONV_WIDTH), F32)
        qbuf[0:QKV_HALO, :] = jnp.zeros((QKV_HALO, QKV_DIM), F32)
        qbuf[QKV_HALO + tm:, :] = jnp.zeros((SUBLANES, QKV_DIM), F32)

    h = _rms_rows(x_ref[0], gain_ref[...]).astype(BF16)

    c = _dot(h, w_ref[:, C_CONV:C_QKV])
    ubuf[CONV_HALO:CONV_HALO + tm, :] = c[:, :CONV_WIDTH] * _sigmoid(c[:, CONV_WIDTH:])
    first = CONV_HALO - (CONV_KERNEL - 1)
    acc = jnp.zeros((tm, CONV_WIDTH), F32) + dwb_ref[...]
    for r in range(SUBLANES):
        part = None
        for j in range(CONV_KERNEL):
            s = first + j
            if s % SUBLANES != r:
                continue
            term = ubuf[s - r:s - r + ext, :] * dww_ref[j:j + 1, :]
            part = term if part is None else part + term
        if part is not None:
            acc = acc + _shift_up(part, r)[0:tm]
    ubuf[0:CONV_HALO, :] = ubuf[tm:tm + CONV_HALO, :]
    mu = jnp.mean(acc, axis=-1, keepdims=True)
    xc = acc - mu
    ln = xc * lax.rsqrt(jnp.mean(xc * xc, axis=-1, keepdims=True) + EPS) * lng_ref[...] + lnb_ref[...]
    y_conv = _dot(_silu(ln).astype(BF16), wco_ref[...])
    gate_c = _sigmoid(_dot(h, w_ref[:, C_GATE_C:C_GATE_D]))
    yc_ref[0] = (gate_c * y_conv).astype(yc_ref.dtype)

    qbuf[QKV_HALO:QKV_HALO + tm, :] = _dot(h, w_ref[:, C_QKV:C_ZG])
    first = QKV_HALO - (SHORT_CONV - 1)
    qkv = jnp.zeros((tm, QKV_DIM), F32)
    for j in range(SHORT_CONV):
        s = first + j
        r = s % SUBLANES
        term = qbuf[s - r:s - r + ext, :] * scw_ref[j:j + 1, :]
        qkv = qkv + _shift_up(term, r)[0:tm]
    qbuf[0:QKV_HALO, :] = qbuf[tm:tm + QKV_HALO, :]
    qkv = _silu(qkv)
    q = qkv[:, :KEY_DIM]
    k = qkv[:, KEY_DIM:2 * KEY_DIM]
    ones = ones_ref[...]
    q_ref[0] = q * lax.rsqrt(_dot((q * q).astype(BF16), ones) + EPS) * (HEAD_K ** -0.5)
    k_ref[0] = k * lax.rsqrt(_dot((k * k).astype(BF16), ones) + EPS)
    v_ref[0] = qkv[:, 2 * KEY_DIM:]

    szg_ref[0] = _silu(_dot(h, w_ref[:, C_ZG:C_BETA])).astype(szg_ref.dtype)
    beta_ref[0] = _sigmoid(_dot(h, w_ref[:, C_BETA:C_DECAY]))
    a = _dot(h, w_ref[:, C_DECAY:C_GATE_C]) + dtb_ref[...]
    softplus = jnp.maximum(a, 0.0) + jnp.log1p(jnp.exp(-jnp.abs(a)))
    g_ref[0] = -jnp.exp(alog_ref[...]) * softplus
    sd_ref[0] = _sigmoid(_dot(h, w_ref[:, C_GATE_D:D_IN_WIDE])).astype(sd_ref.dtype)


def _inproj(x, gain, w_wide, dww, dwb, lng, lnb, wco, scw, alog_x, dtb_x, ones_bd, tm):
    b, t, d = x.shape
    grid = (b, t // tm)
    tile = lambda width: pl.BlockSpec((1, tm, width), lambda bi, ti: (bi, ti, 0))
    out_shape = (
        jax.ShapeDtypeStruct((b, t, D_MODEL), BF16),
        jax.ShapeDtypeStruct((b, t, KEY_DIM), F32),
        jax.ShapeDtypeStruct((b, t, KEY_DIM), F32),
        jax.ShapeDtypeStruct((b, t, VAL_DIM), F32),
        jax.ShapeDtypeStruct((b, t, KEY_DIM), F32),
        jax.ShapeDtypeStruct((b, t, KEY_DIM), F32),
        jax.ShapeDtypeStruct((b, t, VAL_DIM), BF16),
        jax.ShapeDtypeStruct((b, t, D_MODEL), BF16),
    )
    return pl.pallas_call(
        _inproj_kernel,
        grid=grid,
        in_specs=[
            tile(d),
            _const_spec((1, d)),
            _const_spec((d, D_IN_WIDE)),
            _const_spec((CONV_KERNEL, CONV_WIDTH)),
            _const_spec((1, CONV_WIDTH)),
            _const_spec((1, CONV_WIDTH)),
            _const_spec((1, CONV_WIDTH)),
            _const_spec((CONV_WIDTH, D_MODEL)),
            _const_spec((SHORT_CONV, QKV_DIM)),
            _const_spec((1, KEY_DIM)),
            _const_spec((1, KEY_DIM)),
            _const_spec((KEY_DIM, KEY_DIM)),
        ],
        out_specs=(tile(D_MODEL), tile(KEY_DIM), tile(KEY_DIM), tile(VAL_DIM), tile(KEY_DIM),
                   tile(KEY_DIM), tile(VAL_DIM), tile(D_MODEL)),
        out_shape=out_shape,
        scratch_shapes=[
            pltpu.VMEM((CONV_HALO + tm + SUBLANES, CONV_WIDTH), F32),
            pltpu.VMEM((QKV_HALO + tm + SUBLANES, QKV_DIM), F32),
        ],
        compiler_params=pltpu.CompilerParams(
            dimension_semantics=("arbitrary", "arbitrary"), vmem_limit_bytes=VMEM_LIMIT_BYTES),
        name="inproj",
    )(x, gain, w_wide, dww, dwb, lng, lnb, wco, scw, alog_x, dtb_x, ones_bd)


def _gdn_kernel(q_ref, k_ref, v_ref, b_ref, g_ref, tri_ref, bdm_ref, o_ref, s_ref):
    nb, tc = q_ref.shape[0], q_ref.shape[1]
    n_half = KEY_DIM // HALF

    @pl.when(pl.program_id(0) == 0)
    def _():
        s_ref[...] = jnp.zeros(s_ref.shape, F32)

    row = lax.broadcasted_iota(I32, (CHUNK, HALF), 0)
    col = lax.broadcasted_iota(I32, (CHUNK, HALF), 1) & (CHUNK - 1)
    diag = row == col
    causal = row >= col
    strict = row > col
    tri = tri_ref[...]
    bdm = bdm_ref[...]
    bdm_f32 = bdm.astype(F32)

    def bd(m):
        mb = m.astype(BF16)
        return jnp.concatenate([mb, mb, mb, mb], axis=0) * bdm

    chains = [(c, bi, hf) for c in range(tc // CHUNK) for bi in range(nb) for hf in range(n_half)]

    pre = {}
    for ch in chains:
        c, bi, hf = ch
        rows = slice(c * CHUNK, (c + 1) * CHUNK)
        lanes = slice(hf * HALF, (hf + 1) * HALF)
        q = q_ref[bi, rows, lanes]
        k = k_ref[bi, rows, lanes]
        v = v_ref[bi, rows, lanes]
        bx = b_ref[bi, rows, lanes]
        gx = g_ref[bi, rows, lanes]
        g1 = gx.astype(BF16)
        r1 = gx - g1.astype(F32)
        g2 = r1.astype(BF16)
        g3 = (r1 - g2.astype(F32)).astype(BF16)
        gc3 = _dot(tri, jnp.concatenate([g1, g2, g3], axis=1))
        gc = gc3[:, :HALF] + gc3[:, HALF:2 * HALF] + gc3[:, 2 * HALF:]
        gl = gc[CHUNK - 1:CHUNK, :]
        eg = jnp.exp(gc)
        kb = k * bx
        gc_t = jnp.sum(jnp.where(diag, gc, 0.0), axis=0, keepdims=True)
        decay = jnp.where(causal, jnp.exp(jnp.minimum(gc - gc_t, 0.0)), 0.0)
        s1 = _dot_nt(jnp.concatenate([k, q], axis=0).astype(BF16), bd(k))
        p0 = jnp.where(strict, -(s1[:CHUNK] * bx * decay), 0.0)
        pre[ch] = dict(
            p0=p0, qkd=(s1[CHUNK:] * decay).astype(BF16), vb=v * bx, w_in=kb * eg,
            qd=q * eg, kd=(k * jnp.exp(gl - gc)).astype(BF16), gle=jnp.exp(gl))

    tq = {}
    for ch in chains:
        p0 = pre[ch]["p0"]
        tq[ch] = (jnp.where(diag, 1.0, 0.0) + p0, _dot(p0.astype(BF16), bd(p0)))
    for lvl in range(1, 6):
        for ch in chains:
            t, qp = tq[ch]
            if lvl < 5:
                both = _dot(jnp.concatenate([qp, t], axis=0).astype(BF16), bd(qp))
                tq[ch] = (t + both[CHUNK:], both[:CHUNK])
            else:
                tq[ch] = (t + _dot(t.astype(BF16), bd(qp)), None)

    uw = {}
    for ch in chains:
        tb = tq[ch][0].astype(BF16)
        uw[ch] = (_dot(tb, bd(pre[ch]["vb"])), _dot(tb, bd(pre[ch]["w_in"])))

    for bi in range(nb):
        for hf in range(n_half):
            s = s_ref[bi * n_half + hf]
            for c in range(tc // CHUNK):
                ch = (c, bi, hf)
                u, w = uw[ch]
                pm = _dot(jnp.concatenate([w, pre[ch]["qd"]], axis=0).astype(BF16), s.astype(BF16))
                v_new = u - pm[:CHUNK]
                o = pm[CHUNK:] + _dot(pre[ch]["qkd"], bd(v_new))
                o_ref[bi, c * CHUNK:(c + 1) * CHUNK, hf * HALF:(hf + 1) * HALF] = o
                s = s * pre[ch]["gle"] + bdm_f32 * _dot_tn(pre[ch]["kd"], v_new.astype(BF16))
            s_ref[bi * n_half + hf] = s


def _gdn(q, k, v, beta_x, g_x, tri, bdm, tc):
    b, t, _ = q.shape
    spec = pl.BlockSpec((b, tc, KEY_DIM), lambda ti: (0, ti, 0))
    return pl.pallas_call(
        _gdn_kernel,
        grid=(t // tc,),
        in_specs=[spec] * 5 + [_const_spec((CHUNK, CHUNK)), _const_spec((HALF, HALF))],
        out_specs=spec,
        out_shape=jax.ShapeDtypeStruct((b, t, VAL_DIM), F32),
        scratch_shapes=[pltpu.VMEM((b * (KEY_DIM // HALF), HALF, HALF), F32)],
        compiler_params=pltpu.CompilerParams(
            dimension_semantics=("arbitrary",), vmem_limit_bytes=VMEM_LIMIT_BYTES),
        name="gdn",
    )(q, k, v, beta_x, g_x, tri, bdm)


def _merge_kernel(x_ref, yc_ref, o_ref, szg_ref, sd_ref, ng_ref, wgo_ref, wo_ref, fg_ref,
                  wr_ref, br_ref, ones_ref, eye_ref,
                  xo_ref, eid_ref, wts_ref):
    tm = x_ref.shape[0]
    o = o_ref[...]
    ms = _dot((o * o).astype(BF16), ones_ref[...]) * (1.0 / HEAD_V)
    on = o * lax.rsqrt(ms + EPS) * ng_ref[...] * szg_ref[...].astype(F32)
    y_delta = _dot(on.astype(BF16), wgo_ref[...])
    y = yc_ref[...].astype(F32) + sd_ref[...].astype(F32) * y_delta
    x_new = x_ref[...] + _dot(y.astype(BF16), wo_ref[...])
    xo_ref[...] = x_new

    h2 = _rms_rows(x_new, fg_ref[...]).astype(BF16)
    logits = _dot_nt(wr_ref[...], h2) + br_ref[...]
    gl = logits[N_EXPERTS:N_EXPERTS + SUBLANES]
    grow = lax.broadcasted_iota(I32, (SUBLANES, tm), 0)
    gl = jnp.where(grow < N_GROUPS, gl, -jnp.inf)
    gmax = jnp.max(gl, axis=0, keepdims=True)
    group_p = 1.0 / jnp.sum(jnp.exp(gl - gmax), axis=0, keepdims=True)
    gi = jnp.min(jnp.where(gl == gmax, grow, SUBLANES), axis=0, keepdims=True)
    sel = jnp.zeros((EXPERTS_PER_GROUP, tm), F32)
    for g in range(N_GROUPS):
        sel = jnp.where(gi == g, logits[g * EXPERTS_PER_GROUP:(g + 1) * EXPERTS_PER_GROUP], sel)
    erow = lax.broadcasted_iota(I32, (EXPERTS_PER_GROUP, tm), 0)
    m1 = jnp.max(sel, axis=0, keepdims=True)
    i1 = jnp.min(jnp.where(sel == m1, erow, EXPERTS_PER_GROUP), axis=0, keepdims=True)
    sel2 = jnp.where(erow == i1, -jnp.inf, sel)
    m2 = jnp.max(sel2, axis=0, keepdims=True)
    i2 = jnp.min(jnp.where(sel2 == m2, erow, EXPERTS_PER_GROUP), axis=0, keepdims=True)
    e21 = jnp.exp(m2 - m1)
    w1 = group_p / (1.0 + e21)
    w2 = w1 * e21
    eid_ref[...] = jnp.concatenate(
        [gi * EXPERTS_PER_GROUP + i1, gi * EXPERTS_PER_GROUP + i2], axis=0)

    wrow = jnp.concatenate([w1, w2], axis=0)
    p1 = wrow.astype(BF16)
    r1 = wrow - p1.astype(F32)
    p2 = r1.astype(BF16)
    p3 = (r1 - p2.astype(F32)).astype(BF16)
    parts = jnp.concatenate([p1, p2, p3, jnp.zeros((2, tm), BF16)], axis=0)
    cols = _dot_nt(eye_ref[...], parts)
    wcol = cols[:, 0:2] + cols[:, 2:4] + cols[:, 4:6]
    wts_ref[...] = jnp.concatenate([wcol, jnp.zeros((tm, SUBLANES - 2), F32)], axis=1)


def _merge(x2, yc, o, szg, sd, ng_x, wgo, wo, fg, wr_t, br_t, ones_bd, eye, tm):
    n, d = x2.shape
    grid = (n // tm,)
    tile = lambda width: pl.BlockSpec((tm, width), lambda i: (i, 0))
    return pl.pallas_call(
        _merge_kernel,
        grid=grid,
        in_specs=[
            tile(d), tile(d), tile(VAL_DIM), tile(VAL_DIM), tile(d),
            _const_spec((1, VAL_DIM)),
            _const_spec((VAL_DIM, d)),
            _const_spec((d, d)),
            _const_spec((1, d)),
            _const_spec((ROUTER_ROWS, d)),
            _const_spec((ROUTER_ROWS, 1)),
            _const_spec((VAL_DIM, VAL_DIM)),
            _const_spec((tm, tm)),
        ],
        out_specs=(tile(d),
                   pl.BlockSpec((2, tm), lambda i: (0, i)),
                   pl.BlockSpec((tm, SUBLANES), lambda i: (i, 0))),
        out_shape=(jax.ShapeDtypeStruct((n, d), F32),
                   jax.ShapeDtypeStruct((2, n), I32),
                   jax.ShapeDtypeStruct((n, SUBLANES), F32)),
        compiler_params=pltpu.CompilerParams(
            dimension_semantics=("arbitrary",), vmem_limit_bytes=VMEM_LIMIT_BYTES),
        name="merge",
    )(x2, yc, o, szg, sd, ng_x, wgo, wo, fg, wr_t, br_t, ones_bd, eye)


def _plan_kernel(eid_ref, su_ref, lt_ref, pos_ref, te_ref, carry, offs):
    tp = eid_ref.shape[1]
    ph = pl.program_id(0)
    i = pl.program_id(1)

    @pl.when(jnp.logical_and(ph == 0, i == 0))
    def _():
        carry[...] = jnp.zeros(carry.shape, F32)

    @pl.when(jnp.logical_and(ph == 1, i == 0))
    def _():
        tiles = jnp.floor((carry[...] + (ROW_TILE - 1)) * (1.0 / ROW_TILE))
        hi = jnp.floor(tiles * (1.0 / SPLIT))
        lo = tiles - hi * SPLIT
        ex = _dot(lt_ref[...], jnp.concatenate([hi, lo], axis=1).astype(BF16))
        start = ex[:, :LANES] * SPLIT + ex[:, LANES:]
        offs[...] = start * ROW_TILE
        end = (start + tiles)[:, 0:1]
        tile_id = lax.broadcasted_iota(I32, (N_EXPERTS, te_ref.shape[1]), 1).astype(F32)
        te_ref[...] = jnp.sum(jnp.where(tile_id >= end, 1.0, 0.0), axis=0, keepdims=True).astype(I32)
        carry[...] = jnp.zeros(carry.shape, F32)

    e = eid_ref[...]
    rows = lax.broadcasted_iota(I32, (N_EXPERTS, tp), 0)
    hit0 = rows == e[0:1]
    hit1 = rows == e[1:2]
    hits = jnp.where(hit0 | hit1, 1.0, 0.0)

    @pl.when(ph == 1)
    def _():
        row = _dot(hits.astype(BF16), su_ref[...]) + (carry[:, 0:1] + offs[:, 0:1])
        r0 = jnp.sum(jnp.where(hit0, row, 0.0), axis=0, keepdims=True)
        r1 = jnp.sum(jnp.where(hit1, row, 0.0), axis=0, keepdims=True)
        pos_ref[...] = jnp.concatenate([r0, r1], axis=0).astype(I32)

    carry[...] = carry[...] + jnp.sum(hits, axis=1, keepdims=True)


def _plan(eid, su, lt, tp, te_pad):
    n = eid.shape[1]
    return pl.pallas_call(
        _plan_kernel,
        grid=(2, n // tp),
        in_specs=[pl.BlockSpec((2, tp), lambda ph, i: (0, i)),
                  _const_spec((tp, tp)), _const_spec((N_EXPERTS, N_EXPERTS))],
        out_specs=(pl.BlockSpec((2, tp), lambda ph, i: (0, i * ph)),
                   pl.BlockSpec((1, te_pad), lambda ph, i: (0, 0))),
        out_shape=(jax.ShapeDtypeStruct((2, n), I32),
                   jax.ShapeDtypeStruct((1, te_pad), I32)),
        scratch_shapes=[pltpu.VMEM((N_EXPERTS, LANES), F32), pltpu.VMEM((N_EXPERTS, LANES), F32)],
        compiler_params=pltpu.CompilerParams(dimension_semantics=("arbitrary", "arbitrary")),
        name="plan",
    )(eid, su, lt)


def _invert_kernel(pos_ref, inv_ref, *, n_tokens, tb, clear_per_step):
    ph = pl.program_id(0)
    i = pl.program_id(1)
    n_rows = inv_ref.shape[0]

    @pl.when(ph == 0)
    def _():
        def clear(p, carry):
            inv_ref[jnp.minimum(i * clear_per_step + p, n_rows - 1)] = 0
            return carry
        lax.fori_loop(0, clear_per_step, clear, 0, unroll=8)

    @pl.when(ph == 1)
    def _():
        def scatter(t, carry):
            tok = i * tb + t
            inv_ref[pos_ref[tok]] = tok
            inv_ref[pos_ref[n_tokens + tok]] = tok
            return carry
        lax.fori_loop(0, tb, scatter, 0, unroll=4)


def _invert(pos_flat, n_rows, tb):
    n = pos_flat.shape[0] // 2
    steps = n // tb
    grid_spec = pltpu.PrefetchScalarGridSpec(
        num_scalar_prefetch=1,
        grid=(2, steps),
        in_specs=[],
        out_specs=pl.BlockSpec(memory_space=pltpu.SMEM),
    )
    return pl.pallas_call(
        functools.partial(_invert_kernel, n_tokens=n, tb=tb, clear_per_step=-(-n_rows // steps)),
        grid_spec=grid_spec,
        out_shape=jax.ShapeDtypeStruct((n_rows,), I32),
        compiler_params=pltpu.CompilerParams(dimension_semantics=("arbitrary", "arbitrary")),
        name="invert",
    )(pos_flat)


def _expert_kernel(te_ref, inv, x_hbm, fg_ref, wg_ref, wu_ref, wd_ref, ys_ref, xbuf, sems):
    i = pl.program_id(0)
    n_tiles = pl.num_programs(0)

    def gather(tile, slot):
        def issue(r, carry):
            tok = inv[tile * ROW_TILE + r]
            pltpu.make_async_copy(x_hbm.at[pl.ds(tok, 1)], xbuf.at[slot, pl.ds(r, 1)],
                                  sems.at[slot]).start()
            return carry
        lax.fori_loop(0, ROW_TILE, issue, 0, unroll=8)

    def gather_wait(slot):
        pltpu.make_async_copy(x_hbm.at[pl.ds(0, ROW_TILE)], xbuf.at[slot], sems.at[slot]).wait()

    @pl.when(jnp.logical_and(i == 0, te_ref[0] < N_EXPERTS))
    def _():
        gather(0, 0)

    nxt = jnp.minimum(i + 1, n_tiles - 1)

    @pl.when(jnp.logical_and(i + 1 < n_tiles, te_ref[nxt] < N_EXPERTS))
    def _():
        gather(i + 1, (i + 1) % 2)

    live = te_ref[i] < N_EXPERTS

    @pl.when(live)
    def _():
        slot = i % 2
        gather_wait(slot)
        h = _rms_rows(xbuf[slot], fg_ref[...]).astype(BF16)
        hid = _silu(_dot(h, wg_ref[0])) * _dot(h, wu_ref[0])
        ys_ref[...] = _dot(hid.astype(BF16), wd_ref[0])

    @pl.when(jnp.logical_not(live))
    def _():
        ys_ref[...] = jnp.zeros(ys_ref.shape, F32)


def _experts(tile_expert, inv, x2, fg, wg, wu, wd):
    n, d = x2.shape
    n_rows = inv.shape[0]
    expert = lambda i, te, inv: jnp.minimum(te[i], N_EXPERTS - 1)
    grid_spec = pltpu.PrefetchScalarGridSpec(
        num_scalar_prefetch=2,
        grid=(n_rows // ROW_TILE,),
        in_specs=[
            pl.BlockSpec(memory_space=pl.ANY),
            pl.BlockSpec((1, d), lambda i, te, inv: (0, 0)),
            pl.BlockSpec((1, d, D_EXPERT), lambda i, te, inv: (expert(i, te, inv), 0, 0)),
            pl.BlockSpec((1, d, D_EXPERT), lambda i, te, inv: (expert(i, te, inv), 0, 0)),
            pl.BlockSpec((1, D_EXPERT, d), lambda i, te, inv: (expert(i, te, inv), 0, 0)),
        ],
        out_specs=pl.BlockSpec((ROW_TILE, d), lambda i, te, inv: (i, 0)),
        scratch_shapes=[
            pltpu.VMEM((2, ROW_TILE, d), F32),
            pltpu.SemaphoreType.DMA((2,)),
        ],
    )
    return pl.pallas_call(
        _expert_kernel,
        grid_spec=grid_spec,
        out_shape=jax.ShapeDtypeStruct((n_rows, d), F32),
        compiler_params=pltpu.CompilerParams(
            dimension_semantics=("arbitrary",), vmem_limit_bytes=VMEM_LIMIT_BYTES),
        name="experts",
    )(tile_expert, inv, x2, fg, wg, wu, wd)


def _combine_kernel(pos_ref, x_ref, w_ref, fin_ref, ys_hbm, out_ref, ybuf, sems,
                    *, n_tokens, tc, final_norm):
    base = pl.program_id(0) * tc

    def issue(r, carry):
        for slot in range(2):
            p = pos_ref[slot * n_tokens + base + r]
            pltpu.make_async_copy(ys_hbm.at[pl.ds(p, 1)], ybuf.at[slot, pl.ds(r, 1)],
                                  sems.at[slot]).start()
        return carry

    lax.fori_loop(0, tc, issue, 0, unroll=8)
    for slot in range(2):
        pltpu.make_async_copy(ys_hbm.at[pl.ds(0, tc)], ybuf.at[slot], sems.at[slot]).wait()

    w = w_ref[...]
    out = x_ref[...] + w[:, 0:1] * ybuf[0] + w[:, 1:2] * ybuf[1]
    if final_norm:
        out = _rms_rows(out, fin_ref[...])
    out_ref[...] = out


def _combine(pos_flat, x2, wts, fin_gain, ys, tc, final_norm):
    n, d = x2.shape
    grid_spec = pltpu.PrefetchScalarGridSpec(
        num_scalar_prefetch=1,
        grid=(n // tc,),
        in_specs=[
            pl.BlockSpec((tc, d), lambda i, pos: (i, 0)),
            pl.BlockSpec((tc, SUBLANES), lambda i, pos: (i, 0)),
            pl.BlockSpec((1, d), lambda i, pos: (0, 0)),
            pl.BlockSpec(memory_space=pl.ANY),
        ],
        out_specs=pl.BlockSpec((tc, d), lambda i, pos: (i, 0)),
        scratch_shapes=[pltpu.VMEM((2, tc, d), F32), pltpu.SemaphoreType.DMA((2,))],
    )
    return pl.pallas_call(
        functools.partial(_combine_kernel, n_tokens=n, tc=tc, final_norm=final_norm),
        grid_spec=grid_spec,
        out_shape=jax.ShapeDtypeStruct((n, d), F32),
        compiler_params=pltpu.CompilerParams(dimension_semantics=("arbitrary",)),
        name="combine",
    )(pos_flat, x2, wts, fin_gain, ys)


def _tile(n, target):
    t = min(n, target)
    while n % t:
        t //= 2
    return t


def _widen_w_in(w):
    o = 0
    conv = w[:, o:o + 2 * CONV_WIDTH]; o += 2 * CONV_WIDTH
    qkv = w[:, o:o + QKV_DIM]; o += QKV_DIM
    zg = w[:, o:o + VAL_DIM]; o += VAL_DIM
    wb = w[:, o:o + N_HEADS]; o += N_HEADS
    wa = w[:, o:o + N_HEADS]; o += N_HEADS
    gc = w[:, o:o + D_MODEL]; o += D_MODEL
    gd = w[:, o:o + D_MODEL]
    return jnp.concatenate(
        [conv, qkv, zg, jnp.repeat(wb, HEAD_K, axis=1), jnp.repeat(wa, HEAD_K, axis=1), gc, gd],
        axis=1).astype(BF16)


def kernel(x, norm_mix_gain, w_in, conv_dw_w, conv_dw_b, conv_ln_gain, conv_ln_bias, w_conv_out,
           gdn_conv_w, gdn_a_log, gdn_dt_bias, gdn_norm_gain, w_gdn_out, w_out,
           norm_ffn_gain, w_router_group, b_router_group, w_router_expert, b_router_expert,
           w_expert_gate, w_expert_up, w_expert_down, final_norm_gain):
    b, t, d = x.shape
    n = b * t
    depth = w_in.shape[0]
    tm = _tile(t, 512)
    tc_gdn = _tile(t, CHUNK)
    tm_merge = _tile(n, 512)
    tp = _tile(n, 512)
    tc_comb = _tile(n, 256)
    n_rows = -(-(2 * n + N_EXPERTS * (ROW_TILE - 1)) // ROW_TILE) * ROW_TILE
    te_pad = -(-(n_rows // ROW_TILE) // LANES) * LANES

    head_of_lane = jnp.arange(KEY_DIM) // HEAD_K
    ones_bd = (head_of_lane[:, None] == head_of_lane[None, :]).astype(BF16)
    bdm = ones_bd[:HALF, :HALF]
    tri = (jnp.arange(CHUNK)[:, None] >= jnp.arange(CHUNK)[None, :]).astype(BF16)
    eye = jnp.eye(tm_merge, dtype=BF16)
    su = (jnp.arange(tp)[:, None] < jnp.arange(tp)[None, :]).astype(BF16)
    lt = (jnp.arange(N_EXPERTS)[:, None] > jnp.arange(N_EXPERTS)[None, :]).astype(BF16)
    row = lambda a: a.reshape(1, -1).astype(F32)

    for l in range(depth):
        yc, q, k, v, beta_x, g_x, szg, sd = _inproj(
            x, row(norm_mix_gain[l]), _widen_w_in(w_in[l]), conv_dw_w[l], row(conv_dw_b[l]),
            row(conv_ln_gain[l]), row(conv_ln_bias[l]), w_conv_out[l].astype(BF16), gdn_conv_w[l],
            row(jnp.repeat(gdn_a_log[l], HEAD_K)), row(jnp.repeat(gdn_dt_bias[l], HEAD_K)),
            ones_bd, tm)
        o = _gdn(q, k, v, beta_x, g_x, tri, bdm, tc_gdn)

        wr_t = jnp.concatenate(
            [w_router_expert[l].T, w_router_group[l].T,
             jnp.zeros((ROUTER_ROWS - N_EXPERTS - N_GROUPS, d), F32)], axis=0).astype(BF16)
        br_t = jnp.concatenate(
            [b_router_expert[l], b_router_group[l],
             jnp.zeros((ROUTER_ROWS - N_EXPERTS - N_GROUPS,), F32)]).reshape(ROUTER_ROWS, 1)
        x_mid, eid, wts = _merge(
            x.reshape(n, d), yc.reshape(n, d), o.reshape(n, VAL_DIM), szg.reshape(n, VAL_DIM),
            sd.reshape(n, d), row(jnp.tile(gdn_norm_gain[l], N_HEADS)), w_gdn_out[l].astype(BF16),
            w_out[l].astype(BF16), row(norm_ffn_gain[l]), wr_t, br_t, ones_bd, eye, tm_merge)

        pos, tile_expert = _plan(eid, su, lt, tp, te_pad)
        pos_flat = pos.reshape(2 * n)
        inv = _invert(pos_flat, n_rows, tp)
        ys = _experts(tile_expert.reshape(te_pad), inv, x_mid, row(norm_ffn_gain[l]),
                      w_expert_gate[l].astype(BF16), w_expert_up[l].astype(BF16),
                      w_expert_down[l].astype(BF16))
        last = l == depth - 1
        x = _combine(pos_flat, x_mid, wts, row(final_norm_gain), ys, tc_comb, last
                     ).reshape(b, t, d)
    return x
```

```python
import functools

import jax
import jax.numpy as jnp
from jax import lax
from jax.experimental import pallas as pl
from jax.experimental.pallas import tpu as pltpu

F32 = jnp.float32
BF16 = jnp.bfloat16
I32 = jnp.int32

D_MODEL = 1024
CONV_WIDTH = 512
CONV_KERNEL = 31
N_HEADS = 8
HEAD_K = 64
HEAD_V = 64
KEY_DIM = N_HEADS * HEAD_K
VAL_DIM = N_HEADS * HEAD_V
QKV_DIM = 2 * KEY_DIM + VAL_DIM
SHORT_CONV = 4
CHUNK = 64
N_GROUPS = 4
EXPERTS_PER_GROUP = 8
N_EXPERTS = N_GROUPS * EXPERTS_PER_GROUP
D_EXPERT = 256
EPS = 1e-6

SUBLANES = 8
LANES = 128
VMEM_LIMIT_BYTES = 56 * 1024 * 1024

CONV_HALO = 32
QKV_HALO = SUBLANES
HALF = 4 * HEAD_K
ROW_TILE = 256
ROUTER_ROWS = 40
SPLIT = 16

C_CONV = 0
C_QKV = C_CONV + 2 * CONV_WIDTH
C_ZG = C_QKV + QKV_DIM
C_BETA = C_ZG + VAL_DIM
C_DECAY = C_BETA + KEY_DIM
C_GATE_C = C_DECAY + KEY_DIM
C_GATE_D = C_GATE_C + D_MODEL
D_IN_WIDE = C_GATE_D + D_MODEL


def _dot(a, b):
    return jnp.dot(a, b, preferred_element_type=F32)


def _dot_nt(a, b):
    return lax.dot_general(a, b, (((1,), (1,)), ((), ())), preferred_element_type=F32)


def _dot_tn(a, b):
    return lax.dot_general(a, b, (((0,), (0,)), ((), ())), preferred_element_type=F32)


def _sigmoid(x):
    return 0.5 * jnp.tanh(0.5 * x) + 0.5


def _silu(x):
    return x * _sigmoid(x)


def _rms_rows(x, gain):
    return x * lax.rsqrt(jnp.mean(x * x, axis=-1, keepdims=True) + EPS) * gain


def _const_spec(shape):
    zeros = (0,) * len(shape)
    return pl.BlockSpec(shape, lambda *_: zeros, pipeline_mode=pl.Buffered(1))


def _rows_to_tiles(ref, val):
    rows = val.shape[0]
    for cb in range(val.shape[1] // LANES):
        ref[pl.ds(cb, rows, stride=SUBLANES), :] = val[:, cb * LANES:(cb + 1) * LANES]


def _rows_from_tiles(ref, rows):
    return jnp.concatenate(
        [ref[pl.ds(cb, rows, stride=SUBLANES), :] for cb in range(SUBLANES)], axis=1)


def _shift_up(a, r):
    return a if r == 0 else pltpu.roll(a, a.shape[0] - r, 0)


def _inproj_kernel(x_ref, gain_ref, w_ref, dww_ref, dwb_ref, lng_ref, lnb_ref, wco_ref,
                   scw_ref, alog_ref, dtb_ref, ones_ref,
                   yc_ref, q_ref, k_ref, v_ref, beta_ref, g_ref, szg_ref, sd_ref,
                   ubuf, qbuf):
    tm = x_ref.shape[1]
    ext = tm + SUBLANES

    @pl.when(pl.program_id(1) == 0)
    def _():
        ubuf[0:CONV_HALO, :] = jnp.zeros((CONV_HALO, CONV_WIDTH), F32)
        ubuf[CONV_HALO + tm:, :] = jnp.zeros((SUBLANES, CONV_WIDTH), F32)
        qbuf[0:QKV_HALO, :] = jnp.zeros((QKV_HALO, QKV_DIM), F32)
        qbuf[QKV_HALO + tm:, :] = jnp.zeros((SUBLANES, QKV_DIM), F32)

    h = _rms_rows(x_ref[0], gain_ref[...]).astype(BF16)

    c = _dot(h, w_ref[:, C_CONV:C_QKV])
    ubuf[CONV_HALO:CONV_HALO + tm, :] = c[:, :CONV_WIDTH] * _sigmoid(c[:, CONV_WIDTH:])
    first = CONV_HALO - (CONV_KERNEL - 1)
    acc = jnp.zeros((tm, CONV_WIDTH), F32) + dwb_ref[...]
    for r in range(SUBLANES):
        part = None
        for j in range(CONV_KERNEL):
            s = first + j
            if s % SUBLANES != r:
                continue
            term = ubuf[s - r:s - r + ext, :] * dww_ref[j:j + 1, :]
            part = term if part is None else part + term
        if part is not None:
            acc = acc + _shift_up(part, r)[0:tm]
    ubuf[0:CONV_HALO, :] = ubuf[tm:tm + CONV_HALO, :]
    mu = jnp.mean(acc, axis=-1, keepdims=True)
    xc = acc - mu
    ln = xc * lax.rsqrt(jnp.mean(xc * xc, axis=-1, keepdims=True) + EPS) * lng_ref[...] + lnb_ref[...]
    y_conv = _dot(_silu(ln).astype(BF16), wco_ref[...])
    gate_c = _sigmoid(_dot(h, w_ref[:, C_GATE_C:C_GATE_D]))
    yc_ref[0] = (gate_c * y_conv).astype(yc_ref.dtype)

    qbuf[QKV_HALO:QKV_HALO + tm, :] = _dot(h, w_ref[:, C_QKV:C_ZG])
    first = QKV_HALO - (SHORT_CONV - 1)
    qkv = jnp.zeros((tm, QKV_DIM), F32)
    for j in range(SHORT_CONV):
        s = first + j
        r = s % SUBLANES
        term = qbuf[s - r:s - r + ext, :] * scw_ref[j:j + 1, :]
        qkv = qkv + _shift_up(term, r)[0:tm]
    qbuf[0:QKV_HALO, :] = qbuf[tm:tm + QKV_HALO, :]
    qkv = _silu(qkv)
    q = qkv[:, :KEY_DIM]
    k = qkv[:, KEY_DIM:2 * KEY_DIM]
    ones = ones_ref[...]
    q_ref[0] = q * lax.rsqrt(_dot((q * q).astype(BF16), ones) + EPS) * (HEAD_K ** -0.5)
    k_ref[0] = k * lax.rsqrt(_dot((k * k).astype(BF16), ones) + EPS)
    v_ref[0] = qkv[:, 2 * KEY_DIM:]

    szg_ref[0] = _silu(_dot(h, w_ref[:, C_ZG:C_BETA])).astype(szg_ref.dtype)
    beta_ref[0] = _sigmoid(_dot(h, w_ref[:, C_BETA:C_DECAY]))
    a = _dot(h, w_ref[:, C_DECAY:C_GATE_C]) + dtb_ref[...]
    softplus = jnp.maximum(a, 0.0) + jnp.log1p(jnp.exp(-jnp.abs(a)))
    g_ref[0] = -jnp.exp(alog_ref[...]) * softplus
    sd_ref[0] = _sigmoid(_dot(h, w_ref[:, C_GATE_D:D_IN_WIDE])).astype(sd_ref.dtype)


def _inproj(x, gain, w_wide, dww, dwb, lng, lnb, wco, scw, alog_x, dtb_x, ones_bd, tm):
    b, t, d = x.shape
    grid = (b, t // tm)
    tile = lambda width: pl.BlockSpec((1, tm, width), lambda bi, ti: (bi, ti, 0))
    out_shape = (
        jax.ShapeDtypeStruct((b, t, D_MODEL), BF16),
        jax.ShapeDtypeStruct((b, t, KEY_DIM), F32),
        jax.ShapeDtypeStruct((b, t, KEY_DIM), F32),
        jax.ShapeDtypeStruct((b, t, VAL_DIM), F32),
        jax.ShapeDtypeStruct((b, t, KEY_DIM), F32),
        jax.ShapeDtypeStruct((b, t, KEY_DIM), F32),
        jax.ShapeDtypeStruct((b, t, VAL_DIM), BF16),
        jax.ShapeDtypeStruct((b, t, D_MODEL), BF16),
    )
    return pl.pallas_call(
        _inproj_kernel,
        grid=grid,
        in_specs=[
            tile(d),
            _const_spec((1, d)),
            _const_spec((d, D_IN_WIDE)),
            _const_spec((CONV_KERNEL, CONV_WIDTH)),
            _const_spec((1, CONV_WIDTH)),
            _const_spec((1, CONV_WIDTH)),
            _const_spec((1, CONV_WIDTH)),
            _const_spec((CONV_WIDTH, D_MODEL)),
            _const_spec((SHORT_CONV, QKV_DIM)),
            _const_spec((1, KEY_DIM)),
            _const_spec((1, KEY_DIM)),
            _const_spec((KEY_DIM, KEY_DIM)),
        ],
        out_specs=(tile(D_MODEL), tile(KEY_DIM), tile(KEY_DIM), tile(VAL_DIM), tile(KEY_DIM),
                   tile(KEY_DIM), tile(VAL_DIM), tile(D_MODEL)),
        out_shape=out_shape,
        scratch_shapes=[
            pltpu.VMEM((CONV_HALO + tm + SUBLANES, CONV_WIDTH), F32),
            pltpu.VMEM((QKV_HALO + tm + SUBLANES, QKV_DIM), F32),
        ],
        compiler_params=pltpu.CompilerParams(
            dimension_semantics=("arbitrary", "arbitrary"), vmem_limit_bytes=VMEM_LIMIT_BYTES),
        name="inproj",
    )(x, gain, w_wide, dww, dwb, lng, lnb, wco, scw, alog_x, dtb_x, ones_bd)


def _gdn_kernel(q_ref, k_ref, v_ref, b_ref, g_ref, tri_ref, bdm_ref, o_ref, s_ref):
    nb, tc = q_ref.shape[0], q_ref.shape[1]
    n_half = KEY_DIM // HALF

    @pl.when(pl.program_id(0) == 0)
    def _():
        s_ref[...] = jnp.zeros(s_ref.shape, F32)

    row = lax.broadcasted_iota(I32, (CHUNK, HALF), 0)
    col = lax.broadcasted_iota(I32, (CHUNK, HALF), 1) & (CHUNK - 1)
    diag = row == col
    causal = row >= col
    strict = row > col
    tri = tri_ref[...]
    bdm = bdm_ref[...]
    bdm_f32 = bdm.astype(F32)

    def bd(m):
        mb = m.astype(BF16)
        return jnp.concatenate([mb, mb, mb, mb], axis=0) * bdm

    chains = [(c, bi, hf) for c in range(tc // CHUNK) for bi in range(nb) for hf in range(n_half)]

    pre = {}
    for ch in chains:
        c, bi, hf = ch
        rows = slice(c * CHUNK, (c + 1) * CHUNK)
        lanes = slice(hf * HALF, (hf + 1) * HALF)
        q = q_ref[bi, rows, lanes]
        k = k_ref[bi, rows, lanes]
        v = v_ref[bi, rows, lanes]
        bx = b_ref[bi, rows, lanes]
        gx = g_ref[bi, rows, lanes]
        g1 = gx.astype(BF16)
        r1 = gx - g1.astype(F32)
        g2 = r1.astype(BF16)
        g3 = (r1 - g2.astype(F32)).astype(BF16)
        gc3 = _dot(tri, jnp.concatenate([g1, g2, g3], axis=1))
        gc = gc3[:, :HALF] + gc3[:, HALF:2 * HALF] + gc3[:, 2 * HALF:]
        gl = gc[CHUNK - 1:CHUNK, :]
        eg = jnp.exp(gc)
        kb = k * bx
        gc_t = jnp.sum(jnp.where(diag, gc, 0.0), axis=0, keepdims=True)
        decay = jnp.where(causal, jnp.exp(jnp.minimum(gc - gc_t, 0.0)), 0.0)
        s1 = _dot_nt(jnp.concatenate([k, q], axis=0).astype(BF16), bd(k))
        p0 = jnp.where(strict, -(s1[:CHUNK] * bx * decay), 0.0)
        pre[ch] = dict(
            p0=p0, qkd=(s1[CHUNK:] * decay).astype(BF16), vb=v * bx, w_in=kb * eg,
            qd=q * eg, kd=(k * jnp.exp(gl - gc)).astype(BF16), gle=jnp.exp(gl))

    tq = {}
    for ch in chains:
        p0 = pre[ch]["p0"]
        tq[ch] = (jnp.where(diag, 1.0, 0.0) + p0, _dot(p0.astype(BF16), bd(p0)))
    for lvl in range(1, 6):
        for ch in chains:
            t, qp = tq[ch]
            if lvl < 5:
                both = _dot(jnp.concatenate([qp, t], axis=0).astype(BF16), bd(qp))
                tq[ch] = (t + both[CHUNK:], both[:CHUNK])
            else:
                tq[ch] = (t + _dot(t.astype(BF16), bd(qp)), None)

    uw = {}
    for ch in chains:
        tb = tq[ch][0].astype(BF16)
        uw[ch] = (_dot(tb, bd(pre[ch]["vb"])), _dot(tb, bd(pre[ch]["w_in"])))

    for bi in range(nb):
        for hf in range(n_half):
            s = s_ref[bi * n_half + hf]
            for c in range(tc // CHUNK):
                ch = (c, bi, hf)
                u, w = uw[ch]
                pm = _dot(jnp.concatenate([w, pre[ch]["qd"]], axis=0).astype(BF16), s.astype(BF16))
                v_new = u - pm[:CHUNK]
                o = pm[CHUNK:] + _dot(pre[ch]["qkd"], bd(v_new))
                o_ref[bi, c * CHUNK:(c + 1) * CHUNK, hf * HALF:(hf + 1) * HALF] = o
                s = s * pre[ch]["gle"] + bdm_f32 * _dot_tn(pre[ch]["kd"], v_new.astype(BF16))
            s_ref[bi * n_half + hf] = s


def _gdn(q, k, v, beta_x, g_x, tri, bdm, tc):
    b, t, _ = q.shape
    spec = pl.BlockSpec((b, tc, KEY_DIM), lambda ti: (0, ti, 0))
    return pl.pallas_call(
        _gdn_kernel,
        grid=(t // tc,),
        in_specs=[spec] * 5 + [_const_spec((CHUNK, CHUNK)), _const_spec((HALF, HALF))],
        out_specs=spec,
        out_shape=jax.ShapeDtypeStruct((b, t, VAL_DIM), F32),
        scratch_shapes=[pltpu.VMEM((b * (KEY_DIM // HALF), HALF, HALF), F32)],
        compiler_params=pltpu.CompilerParams(
            dimension_semantics=("arbitrary",), vmem_limit_bytes=VMEM_LIMIT_BYTES),
        name="gdn",
    )(q, k, v, beta_x, g_x, tri, bdm)


def _merge_kernel(x_ref, yc_ref, o_ref, szg_ref, sd_ref, ng_ref, wgo_ref, wo_ref, fg_ref,
                  wr_ref, br_ref, ones_ref, eye_ref,
                  xo_ref, eid_ref, wts_ref):
    tm = x_ref.shape[0]
    o = o_ref[...]
    ms = _dot((o * o).astype(BF16), ones_ref[...]) * (1.0 / HEAD_V)
    on = o * lax.rsqrt(ms + EPS) * ng_ref[...] * szg_ref[...].astype(F32)
    y_delta = _dot(on.astype(BF16), wgo_ref[...])
    y = yc_ref[...].astype(F32) + sd_ref[...].astype(F32) * y_delta
    x_new = x_ref[...] + _dot(y.astype(BF16), wo_ref[...])
    _rows_to_tiles(xo_ref, x_new)

    h2 = _rms_rows(x_new, fg_ref[...]).astype(BF16)
    logits = _dot_nt(wr_ref[...], h2) + br_ref[...]
    gl = logits[N_EXPERTS:N_EXPERTS + SUBLANES]
    grow = lax.broadcasted_iota(I32, (SUBLANES, tm), 0)
    gl = jnp.where(grow < N_GROUPS, gl, -jnp.inf)
    gmax = jnp.max(gl, axis=0, keepdims=True)
    group_p = 1.0 / jnp.sum(jnp.exp(gl - gmax), axis=0, keepdims=True)
    gi = jnp.min(jnp.where(gl == gmax, grow, SUBLANES), axis=0, keepdims=True)
    sel = jnp.zeros((EXPERTS_PER_GROUP, tm), F32)
    for g in range(N_GROUPS):
        sel = jnp.where(gi == g, logits[g * EXPERTS_PER_GROUP:(g + 1) * EXPERTS_PER_GROUP], sel)
    erow = lax.broadcasted_iota(I32, (EXPERTS_PER_GROUP, tm), 0)
    m1 = jnp.max(sel, axis=0, keepdims=True)
    i1 = jnp.min(jnp.where(sel == m1, erow, EXPERTS_PER_GROUP), axis=0, keepdims=True)
    sel2 = jnp.where(erow == i1, -jnp.inf, sel)
    m2 = jnp.max(sel2, axis=0, keepdims=True)
    i2 = jnp.min(jnp.where(sel2 == m2, erow, EXPERTS_PER_GROUP), axis=0, keepdims=True)
    e21 = jnp.exp(m2 - m1)
    w1 = group_p / (1.0 + e21)
    w2 = w1 * e21
    eid_ref[...] = jnp.concatenate(
        [gi * EXPERTS_PER_GROUP + i1, gi * EXPERTS_PER_GROUP + i2], axis=0)

    wrow = jnp.concatenate([w1, w2], axis=0)
    p1 = wrow.astype(BF16)
    r1 = wrow - p1.astype(F32)
    p2 = r1.astype(BF16)
    p3 = (r1 - p2.astype(F32)).astype(BF16)
    parts = jnp.concatenate([p1, p2, p3, jnp.zeros((2, tm), BF16)], axis=0)
    cols = _dot_nt(eye_ref[...], parts)
    wcol = cols[:, 0:2] + cols[:, 2:4] + cols[:, 4:6]
    wts_ref[...] = jnp.concatenate([wcol, jnp.zeros((tm, SUBLANES - 2), F32)], axis=1)


def _merge(x2, yc, o, szg, sd, ng_x, wgo, wo, fg, wr_t, br_t, ones_bd, eye, tm):
    n, d = x2.shape
    grid = (n // tm,)
    tile = lambda width: pl.BlockSpec((tm, width), lambda i: (i, 0))
    return pl.pallas_call(
        _merge_kernel,
        grid=grid,
        in_specs=[
            tile(d), tile(d), tile(VAL_DIM), tile(VAL_DIM), tile(d),
            _const_spec((1, VAL_DIM)),
            _const_spec((VAL_DIM, d)),
            _const_spec((d, d)),
            _const_spec((1, d)),
            _const_spec((ROUTER_ROWS, d)),
            _const_spec((ROUTER_ROWS, 1)),
            _const_spec((VAL_DIM, VAL_DIM)),
            _const_spec((tm, tm)),
        ],
        out_specs=(pl.BlockSpec((tm * SUBLANES, LANES), lambda i: (i, 0)),
                   pl.BlockSpec((2, tm), lambda i: (0, i)),
                   pl.BlockSpec((tm, SUBLANES), lambda i: (i, 0))),
        out_shape=(jax.ShapeDtypeStruct((n * SUBLANES, LANES), F32),
                   jax.ShapeDtypeStruct((2, n), I32),
                   jax.ShapeDtypeStruct((n, SUBLANES), F32)),
        compiler_params=pltpu.CompilerParams(
            dimension_semantics=("arbitrary",), vmem_limit_bytes=VMEM_LIMIT_BYTES),
        name="merge",
    )(x2, yc, o, szg, sd, ng_x, wgo, wo, fg, wr_t, br_t, ones_bd, eye)


def _plan_kernel(eid_ref, su_ref, lt_ref, pos_ref, te_ref, pad_ref, carry, offs, *, n_rows):
    tp = eid_ref.shape[1]
    ph = pl.program_id(0)
    i = pl.program_id(1)

    @pl.when(jnp.logical_and(ph == 0, i == 0))
    def _():
        carry[...] = jnp.zeros(carry.shape, F32)

    @pl.when(jnp.logical_and(ph == 1, i == 0))
    def _():
        count = carry[...]
        tiles = jnp.floor((count + (ROW_TILE - 1)) * (1.0 / ROW_TILE))
        hi = jnp.floor(tiles * (1.0 / SPLIT))
        lo = tiles - hi * SPLIT
        ex = _dot(lt_ref[...], jnp.concatenate([hi, lo], axis=1).astype(BF16))
        start = ex[:, :LANES] * SPLIT + ex[:, LANES:]
        offs[...] = start * ROW_TILE
        end = (start + tiles)[:, 0:1]
        tile_id = lax.broadcasted_iota(I32, (N_EXPERTS, te_ref.shape[1]), 1).astype(F32)
        te_ref[...] = jnp.sum(jnp.where(tile_id >= end, 1.0, 0.0), axis=0, keepdims=True).astype(I32)
        on_lane = (lax.broadcasted_iota(I32, (N_EXPERTS, LANES), 0)
                   == lax.broadcasted_iota(I32, (N_EXPERTS, LANES), 1))
        last = lax.broadcasted_iota(I32, (N_EXPERTS, LANES), 0) == N_EXPERTS - 1
        lo_rows = start * ROW_TILE + count
        hi_rows = jnp.where(last, float(n_rows), (start + tiles) * ROW_TILE)
        pad_ref[0:1, :] = jnp.sum(jnp.where(on_lane, lo_rows, 0.0), axis=0, keepdims=True).astype(I32)
        pad_ref[1:2, :] = jnp.sum(jnp.where(on_lane, hi_rows, 0.0), axis=0, keepdims=True).astype(I32)
        carry[...] = jnp.zeros(carry.shape, F32)

    e = eid_ref[...]
    rows = lax.broadcasted_iota(I32, (N_EXPERTS, tp), 0)
    hit0 = rows == e[0:1]
    hit1 = rows == e[1:2]
    hits = jnp.where(hit0 | hit1, 1.0, 0.0)

    @pl.when(ph == 1)
    def _():
        row = _dot(hits.astype(BF16), su_ref[...]) + (carry[:, 0:1] + offs[:, 0:1])
        r0 = jnp.sum(jnp.where(hit0, row, 0.0), axis=0, keepdims=True)
        r1 = jnp.sum(jnp.where(hit1, row, 0.0), axis=0, keepdims=True)
        pos_ref[...] = jnp.concatenate([r0, r1], axis=0).astype(I32)

    carry[...] = carry[...] + jnp.sum(hits, axis=1, keepdims=True)


def _plan(eid, su, lt, tp, te_pad, n_rows):
    n = eid.shape[1]
    return pl.pallas_call(
        functools.partial(_plan_kernel, n_rows=n_rows),
        grid=(2, n // tp),
        in_specs=[pl.BlockSpec((2, tp), lambda ph, i: (0, i)),
                  _const_spec((tp, tp)), _const_spec((N_EXPERTS, N_EXPERTS))],
        out_specs=(pl.BlockSpec((2, tp), lambda ph, i: (0, i * ph)),
                   pl.BlockSpec((1, te_pad), lambda ph, i: (0, 0)),
                   pl.BlockSpec((2, LANES), lambda ph, i: (0, 0))),
        out_shape=(jax.ShapeDtypeStruct((2, n), I32),
                   jax.ShapeDtypeStruct((1, te_pad), I32),
                   jax.ShapeDtypeStruct((2, LANES), I32)),
        scratch_shapes=[pltpu.VMEM((N_EXPERTS, LANES), F32), pltpu.VMEM((N_EXPERTS, LANES), F32)],
        compiler_params=pltpu.CompilerParams(dimension_semantics=("arbitrary", "arbitrary")),
        name="plan",
    )(eid, su, lt)


def _invert_kernel(pos_ref, pad_ref, inv_ref, *, n_tokens, tb):
    base = pl.program_id(0) * tb

    @pl.when(pl.program_id(0) == 0)
    def _():
        def clear_expert(e, carry):
            def clear(p, c):
                inv_ref[p] = 0
                return c
            return lax.fori_loop(pad_ref[e], pad_ref[LANES + e], clear, carry)
        lax.fori_loop(0, N_EXPERTS, clear_expert, 0)

    def scatter(t, carry):
        tok = base + t
        inv_ref[pos_ref[tok]] = tok
        inv_ref[pos_ref[n_tokens + tok]] = tok
        return carry
    lax.fori_loop(0, tb, scatter, 0, unroll=8)


def _invert(pos_flat, pad_flat, n_rows, tb):
    n = pos_flat.shape[0] // 2
    grid_spec = pltpu.PrefetchScalarGridSpec(
        num_scalar_prefetch=2,
        grid=(n // tb,),
        in_specs=[],
        out_specs=pl.BlockSpec(memory_space=pltpu.SMEM),
    )
    return pl.pallas_call(
        functools.partial(_invert_kernel, n_tokens=n, tb=tb),
        grid_spec=grid_spec,
        out_shape=jax.ShapeDtypeStruct((n_rows,), I32),
        compiler_params=pltpu.CompilerParams(dimension_semantics=("arbitrary",)),
        name="invert",
    )(pos_flat, pad_flat)


def _tile_dma(src_hbm, row, dst, r, sem):
    return pltpu.make_async_copy(
        src_hbm.at[pl.ds(pl.multiple_of(row * SUBLANES, SUBLANES), SUBLANES)],
        dst.at[pl.ds(pl.multiple_of(r * SUBLANES, SUBLANES), SUBLANES)], sem)


def _expert_kernel(te_ref, inv, x_hbm, fg_ref, wg_ref, wu_ref, wd_ref, ys_ref, xbuf, sems):
    i = pl.program_id(0)
    n_tiles = pl.num_programs(0)

    def gather(tile, slot):
        def issue(r, carry):
            _tile_dma(x_hbm, inv[tile * ROW_TILE + r], xbuf.at[slot], r, sems.at[slot]).start()
            return carry
        lax.fori_loop(0, ROW_TILE, issue, 0, unroll=8)

    def gather_wait(slot):
        pltpu.make_async_copy(x_hbm.at[pl.ds(0, ROW_TILE * SUBLANES)], xbuf.at[slot],
                              sems.at[slot]).wait()

    @pl.when(jnp.logical_and(i == 0, te_ref[0] < N_EXPERTS))
    def _():
        gather(0, 0)

    nxt = jnp.minimum(i + 1, n_tiles - 1)

    @pl.when(jnp.logical_and(i + 1 < n_tiles, te_ref[nxt] < N_EXPERTS))
    def _():
        gather(i + 1, (i + 1) % 2)

    live = te_ref[i] < N_EXPERTS

    @pl.when(live)
    def _():
        slot = i % 2
        gather_wait(slot)
        x = _rows_from_tiles(xbuf.at[slot], ROW_TILE)
        h = _rms_rows(x, fg_ref[...]).astype(BF16)
        hid = _silu(_dot(h, wg_ref[0].astype(BF16))) * _dot(h, wu_ref[0].astype(BF16))
        _rows_to_tiles(ys_ref, _dot(hid.astype(BF16), wd_ref[0].astype(BF16)))

    @pl.when(jnp.logical_not(live))
    def _():
        ys_ref[...] = jnp.zeros(ys_ref.shape, F32)


def _experts(tile_expert, inv, x_tiles, fg, wg, wu, wd, layer):
    d = fg.shape[1]
    n_rows = inv.shape[0]
    expert = lambda i, te, inv: (layer, jnp.minimum(te[i], N_EXPERTS - 1), 0, 0)
    grid_spec = pltpu.PrefetchScalarGridSpec(
        num_scalar_prefetch=2,
        grid=(n_rows // ROW_TILE,),
        in_specs=[
            pl.BlockSpec(memory_space=pl.ANY),
            pl.BlockSpec((1, d), lambda i, te, inv: (0, 0)),
            pl.BlockSpec((None, 1, d, D_EXPERT), expert),
            pl.BlockSpec((None, 1, d, D_EXPERT), expert),
            pl.BlockSpec((None, 1, D_EXPERT, d), expert),
        ],
        out_specs=pl.BlockSpec((ROW_TILE * SUBLANES, LANES), lambda i, te, inv: (i, 0)),
        scratch_shapes=[
            pltpu.VMEM((2, ROW_TILE * SUBLANES, LANES), F32),
            pltpu.SemaphoreType.DMA((2,)),
        ],
    )
    return pl.pallas_call(
        _expert_kernel,
        grid_spec=grid_spec,
        out_shape=jax.ShapeDtypeStruct((n_rows * SUBLANES, LANES), F32),
        compiler_params=pltpu.CompilerParams(
            dimension_semantics=("arbitrary",), vmem_limit_bytes=VMEM_LIMIT_BYTES),
        name="experts",
    )(tile_expert, inv, x_tiles, fg, wg, wu, wd)


def _combine_kernel(pos_ref, x_ref, w_ref, fin_ref, ys_hbm, out_ref, ybuf, sems,
                    *, n_tokens, tc, final_norm):
    base = pl.program_id(0) * tc

    def issue(r, carry):
        for slot in range(2):
            p = pos_ref[slot * n_tokens + base + r]
            _tile_dma(ys_hbm, p, ybuf.at[slot], r, sems.at[slot]).start()
        return carry

    lax.fori_loop(0, tc, issue, 0, unroll=8)
    for slot in range(2):
        pltpu.make_async_copy(ys_hbm.at[pl.ds(0, tc * SUBLANES)], ybuf.at[slot],
                              sems.at[slot]).wait()

    w = w_ref[...]
    out = (_rows_from_tiles(x_ref, tc) + w[:, 0:1] * _rows_from_tiles(ybuf.at[0], tc)
           + w[:, 1:2] * _rows_from_tiles(ybuf.at[1], tc))
    if final_norm:
        out = _rms_rows(out, fin_ref[...])
    out_ref[...] = out


def _combine(pos_flat, x_tiles, wts, fin_gain, ys_tiles, tc, final_norm):
    n = wts.shape[0]
    d = fin_gain.shape[1]
    grid_spec = pltpu.PrefetchScalarGridSpec(
        num_scalar_prefetch=1,
        grid=(n // tc,),
        in_specs=[
            pl.BlockSpec((tc * SUBLANES, LANES), lambda i, pos: (i, 0)),
            pl.BlockSpec((tc, SUBLANES), lambda i, pos: (i, 0)),
            pl.BlockSpec((1, d), lambda i, pos: (0, 0)),
            pl.BlockSpec(memory_space=pl.ANY),
        ],
        out_specs=pl.BlockSpec((tc, d), lambda i, pos: (i, 0)),
        scratch_shapes=[pltpu.VMEM((2, tc * SUBLANES, LANES), F32), pltpu.SemaphoreType.DMA((2,))],
    )
    return pl.pallas_call(
        functools.partial(_combine_kernel, n_tokens=n, tc=tc, final_norm=final_norm),
        grid_spec=grid_spec,
        out_shape=jax.ShapeDtypeStruct((n, d), F32),
        compiler_params=pltpu.CompilerParams(dimension_semantics=("arbitrary",)),
        name="combine",
    )(pos_flat, x_tiles, wts, fin_gain, ys_tiles)


def _tile(n, target):
    t = min(n, target)
    while n % t:
        t //= 2
    return t


def _widen_w_in(w):
    o = 0
    conv = w[:, o:o + 2 * CONV_WIDTH]; o += 2 * CONV_WIDTH
    qkv = w[:, o:o + QKV_DIM]; o += QKV_DIM
    zg = w[:, o:o + VAL_DIM]; o += VAL_DIM
    wb = w[:, o:o + N_HEADS]; o += N_HEADS
    wa = w[:, o:o + N_HEADS]; o += N_HEADS
    gc = w[:, o:o + D_MODEL]; o += D_MODEL
    gd = w[:, o:o + D_MODEL]
    return jnp.concatenate(
        [conv, qkv, zg, jnp.repeat(wb, HEAD_K, axis=1), jnp.repeat(wa, HEAD_K, axis=1), gc, gd],
        axis=1).astype(BF16)


def kernel(x, norm_mix_gain, w_in, conv_dw_w, conv_dw_b, conv_ln_gain, conv_ln_bias, w_conv_out,
           gdn_conv_w, gdn_a_log, gdn_dt_bias, gdn_norm_gain, w_gdn_out, w_out,
           norm_ffn_gain, w_router_group, b_router_group, w_router_expert, b_router_expert,
           w_expert_gate, w_expert_up, w_expert_down, final_norm_gain):
    b, t, d = x.shape
    n = b * t
    depth = w_in.shape[0]
    tm = _tile(t, 512)
    tc_gdn = _tile(t, CHUNK)
    tm_merge = _tile(n, 512)
    tp = _tile(n, 512)
    t_inv = _tile(n, 2048)
    tc_comb = _tile(n, 256)
    n_rows = -(-(2 * n + N_EXPERTS * (ROW_TILE - 1)) // ROW_TILE) * ROW_TILE
    te_pad = -(-(n_rows // ROW_TILE) // LANES) * LANES

    head_of_lane = jnp.arange(KEY_DIM) // HEAD_K
    ones_bd = (head_of_lane[:, None] == head_of_lane[None, :]).astype(BF16)
    bdm = ones_bd[:HALF, :HALF]
    tri = (jnp.arange(CHUNK)[:, None] >= jnp.arange(CHUNK)[None, :]).astype(BF16)
    eye = jnp.eye(tm_merge, dtype=BF16)
    su = (jnp.arange(tp)[:, None] < jnp.arange(tp)[None, :]).astype(BF16)
    lt = (jnp.arange(N_EXPERTS)[:, None] > jnp.arange(N_EXPERTS)[None, :]).astype(BF16)
    row = lambda a: a.reshape(1, -1).astype(F32)

    for l in range(depth):
        yc, q, k, v, beta_x, g_x, szg, sd = _inproj(
            x, row(norm_mix_gain[l]), _widen_w_in(w_in[l]), conv_dw_w[l], row(conv_dw_b[l]),
            row(conv_ln_gain[l]), row(conv_ln_bias[l]), w_conv_out[l].astype(BF16), gdn_conv_w[l],
            row(jnp.repeat(gdn_a_log[l], HEAD_K)), row(jnp.repeat(gdn_dt_bias[l], HEAD_K)),
            ones_bd, tm)
        o = _gdn(q, k, v, beta_x, g_x, tri, bdm, tc_gdn)

        wr_t = jnp.concatenate(
            [w_router_expert[l].T, w_router_group[l].T,
             jnp.zeros((ROUTER_ROWS - N_EXPERTS - N_GROUPS, d), F32)], axis=0).astype(BF16)
        br_t = jnp.concatenate(
            [b_router_expert[l], b_router_group[l],
             jnp.zeros((ROUTER_ROWS - N_EXPERTS - N_GROUPS,), F32)]).reshape(ROUTER_ROWS, 1)
        x_mid, eid, wts = _merge(
            x.reshape(n, d), yc.reshape(n, d), o.reshape(n, VAL_DIM), szg.reshape(n, VAL_DIM),
            sd.reshape(n, d), row(jnp.tile(gdn_norm_gain[l], N_HEADS)), w_gdn_out[l].astype(BF16),
            w_out[l].astype(BF16), row(norm_ffn_gain[l]), wr_t, br_t, ones_bd, eye, tm_merge)

        pos, tile_expert, pad_rows = _plan(eid, su, lt, tp, te_pad, n_rows)
        pos_flat = pos.reshape(2 * n)
        inv = _invert(pos_flat, pad_rows.reshape(2 * LANES), n_rows, t_inv)
        ys = _experts(tile_expert.reshape(te_pad), inv, x_mid, row(norm_ffn_gain[l]),
                      w_expert_gate, w_expert_up, w_expert_down, l)
        last = l == depth - 1
        x = _combine(pos_flat, x_mid, wts, row(final_norm_gain), ys, tc_comb, last
                     ).reshape(b, t, d)
    return x
```

```python
import functools

import jax
import jax.numpy as jnp
from jax import lax
from jax.experimental import pallas as pl
from jax.experimental.pallas import tpu as pltpu

F32 = jnp.float32
BF16 = jnp.bfloat16
I32 = jnp.int32

D_MODEL = 1024
CONV_WIDTH = 512
CONV_KERNEL = 31
N_HEADS = 8
HEAD_K = 64
HEAD_V = 64
KEY_DIM = N_HEADS * HEAD_K
VAL_DIM = N_HEADS * HEAD_V
QKV_DIM = 2 * KEY_DIM + VAL_DIM
SHORT_CONV = 4
CHUNK = 64
N_GROUPS = 4
EXPERTS_PER_GROUP = 8
N_EXPERTS = N_GROUPS * EXPERTS_PER_GROUP
D_EXPERT = 256
EPS = 1e-6

SUBLANES = 8
LANES = 128
VMEM_LIMIT_BYTES = 56 * 1024 * 1024

CONV_HALO = 32
QKV_HALO = SUBLANES
HALF = 4 * HEAD_K
ROW_TILE = 256
ROUTER_ROWS = 40
ISSUE_UNROLL = 8
SPLIT = 16

C_CONV = 0
C_QKV = C_CONV + 2 * CONV_WIDTH
C_ZG = C_QKV + QKV_DIM
C_BETA = C_ZG + VAL_DIM
C_DECAY = C_BETA + KEY_DIM
C_GATE_C = C_DECAY + KEY_DIM
C_GATE_D = C_GATE_C + D_MODEL
D_IN_WIDE = C_GATE_D + D_MODEL


def _dot(a, b):
    return jnp.dot(a, b, preferred_element_type=F32)


def _dot_nt(a, b):
    return lax.dot_general(a, b, (((1,), (1,)), ((), ())), preferred_element_type=F32)


def _dot_tn(a, b):
    return lax.dot_general(a, b, (((0,), (0,)), ((), ())), preferred_element_type=F32)


def _sigmoid(x):
    return 0.5 * jnp.tanh(0.5 * x) + 0.5


def _silu(x):
    return x * _sigmoid(x)


def _rms_rows(x, gain):
    return x * lax.rsqrt(jnp.mean(x * x, axis=-1, keepdims=True) + EPS) * gain


def _const_spec(shape):
    zeros = (0,) * len(shape)
    return pl.BlockSpec(shape, lambda *_: zeros, pipeline_mode=pl.Buffered(1))


def _rows_to_tiles(ref, val):
    rows = val.shape[0]
    for cb in range(val.shape[1] // LANES):
        ref[pl.ds(cb, rows, stride=SUBLANES), :] = val[:, cb * LANES:(cb + 1) * LANES]


def _rows_from_tiles(ref, rows):
    return jnp.concatenate(
        [ref[pl.ds(cb, rows, stride=SUBLANES), :] for cb in range(SUBLANES)], axis=1)


def _shift_up(a, r):
    return a if r == 0 else pltpu.roll(a, a.shape[0] - r, 0)


def _inproj_kernel(x_ref, gain_ref, w_ref, dww_ref, dwb_ref, lng_ref, lnb_ref, wco_ref,
                   scw_ref, alog_ref, dtb_ref, ones_ref,
                   yc_ref, q_ref, k_ref, v_ref, beta_ref, g_ref, szg_ref, sd_ref,
                   ubuf, qbuf):
    tm = x_ref.shape[1]
    ext = tm + SUBLANES

    @pl.when(pl.program_id(1) == 0)
    def _():
        ubuf[0:CONV_HALO, :] = jnp.zeros((CONV_HALO, CONV_WIDTH), F32)
        ubuf[CONV_HALO + tm:, :] = jnp.zeros((SUBLANES, CONV_WIDTH), F32)
        qbuf[0:QKV_HALO, :] = jnp.zeros((QKV_HALO, QKV_DIM), F32)
        qbuf[QKV_HALO + tm:, :] = jnp.zeros((SUBLANES, QKV_DIM), F32)

    h = _rms_rows(x_ref[0], gain_ref[...]).astype(BF16)

    c = _dot(h, w_ref[:, C_CONV:C_QKV])
    ubuf[CONV_HALO:CONV_HALO + tm, :] = c[:, :CONV_WIDTH] * _sigmoid(c[:, CONV_WIDTH:])
    first = CONV_HALO - (CONV_KERNEL - 1)
    acc = jnp.zeros((tm, CONV_WIDTH), F32) + dwb_ref[...]
    for r in range(SUBLANES):
        part = None
        for j in range(CONV_KERNEL):
            s = first + j
            if s % SUBLANES != r:
                continue
            term = ubuf[s - r:s - r + ext, :] * dww_ref[j:j + 1, :]
            part = term if part is None else part + term
        if part is not None:
            acc = acc + _shift_up(part, r)[0:tm]
    ubuf[0:CONV_HALO, :] = ubuf[tm:tm + CONV_HALO, :]
    mu = jnp.mean(acc, axis=-1, keepdims=True)
    xc = acc - mu
    ln = xc * lax.rsqrt(jnp.mean(xc * xc, axis=-1, keepdims=True) + EPS) * lng_ref[...] + lnb_ref[...]
    y_conv = _dot(_silu(ln).astype(BF16), wco_ref[...])
    gate_c = _sigmoid(_dot(h, w_ref[:, C_GATE_C:C_GATE_D]))
    yc_ref[0] = (gate_c * y_conv).astype(yc_ref.dtype)

    qbuf[QKV_HALO:QKV_HALO + tm, :] = _dot(h, w_ref[:, C_QKV:C_ZG])
    first = QKV_HALO - (SHORT_CONV - 1)
    qkv = jnp.zeros((tm, QKV_DIM), F32)
    for j in range(SHORT_CONV):
        s = first + j
        r = s % SUBLANES
        term = qbuf[s - r:s - r + ext, :] * scw_ref[j:j + 1, :]
        qkv = qkv + _shift_up(term, r)[0:tm]
    qbuf[0:QKV_HALO, :] = qbuf[tm:tm + QKV_HALO, :]
    qkv = _silu(qkv)
    q = qkv[:, :KEY_DIM]
    k = qkv[:, KEY_DIM:2 * KEY_DIM]
    ones = ones_ref[...]
    q_ref[0] = q * lax.rsqrt(_dot((q * q).astype(BF16), ones) + EPS) * (HEAD_K ** -0.5)
    k_ref[0] = k * lax.rsqrt(_dot((k * k).astype(BF16), ones) + EPS)
    v_ref[0] = qkv[:, 2 * KEY_DIM:]

    szg_ref[0] = _silu(_dot(h, w_ref[:, C_ZG:C_BETA])).astype(szg_ref.dtype)
    beta_ref[0] = _sigmoid(_dot(h, w_ref[:, C_BETA:C_DECAY]))
    a = _dot(h, w_ref[:, C_DECAY:C_GATE_C]) + dtb_ref[...]
    softplus = jnp.maximum(a, 0.0) + jnp.log1p(jnp.exp(-jnp.abs(a)))
    g_ref[0] = -jnp.exp(alog_ref[...]) * softplus
    sd_ref[0] = _sigmoid(_dot(h, w_ref[:, C_GATE_D:D_IN_WIDE])).astype(sd_ref.dtype)


def _inproj(x, gain, w_wide, dww, dwb, lng, lnb, wco, scw, alog_x, dtb_x, ones_bd, tm):
    b, t, d = x.shape
    grid = (b, t // tm)
    tile = lambda width: pl.BlockSpec((1, tm, width), lambda bi, ti: (bi, ti, 0))
    out_shape = (
        jax.ShapeDtypeStruct((b, t, D_MODEL), BF16),
        jax.ShapeDtypeStruct((b, t, KEY_DIM), F32),
        jax.ShapeDtypeStruct((b, t, KEY_DIM), F32),
        jax.ShapeDtypeStruct((b, t, VAL_DIM), F32),
        jax.ShapeDtypeStruct((b, t, KEY_DIM), F32),
        jax.ShapeDtypeStruct((b, t, KEY_DIM), F32),
        jax.ShapeDtypeStruct((b, t, VAL_DIM), BF16),
        jax.ShapeDtypeStruct((b, t, D_MODEL), BF16),
    )
    return pl.pallas_call(
        _inproj_kernel,
        grid=grid,
        in_specs=[
            tile(d),
            _const_spec((1, d)),
            _const_spec((d, D_IN_WIDE)),
            _const_spec((CONV_KERNEL, CONV_WIDTH)),
            _const_spec((1, CONV_WIDTH)),
            _const_spec((1, CONV_WIDTH)),
            _const_spec((1, CONV_WIDTH)),
            _const_spec((CONV_WIDTH, D_MODEL)),
            _const_spec((SHORT_CONV, QKV_DIM)),
            _const_spec((1, KEY_DIM)),
            _const_spec((1, KEY_DIM)),
            _const_spec((KEY_DIM, KEY_DIM)),
        ],
        out_specs=(tile(D_MODEL), tile(KEY_DIM), tile(KEY_DIM), tile(VAL_DIM), tile(KEY_DIM),
                   tile(KEY_DIM), tile(VAL_DIM), tile(D_MODEL)),
        out_shape=out_shape,
        scratch_shapes=[
            pltpu.VMEM((CONV_HALO + tm + SUBLANES, CONV_WIDTH), F32),
            pltpu.VMEM((QKV_HALO + tm + SUBLANES, QKV_DIM), F32),
        ],
        compiler_params=pltpu.CompilerParams(
            dimension_semantics=("arbitrary", "arbitrary"), vmem_limit_bytes=VMEM_LIMIT_BYTES),
        name="inproj",
    )(x, gain, w_wide, dww, dwb, lng, lnb, wco, scw, alog_x, dtb_x, ones_bd)


def _gdn_kernel(q_ref, k_ref, v_ref, b_ref, g_ref, bdm_ref, o_ref, s_ref):
    nb, tc = q_ref.shape[0], q_ref.shape[1]
    n_half = KEY_DIM // HALF

    @pl.when(pl.program_id(0) == 0)
    def _():
        s_ref[...] = jnp.zeros(s_ref.shape, F32)

    row = lax.broadcasted_iota(I32, (CHUNK, HALF), 0)
    col = lax.broadcasted_iota(I32, (CHUNK, HALF), 1) & (CHUNK - 1)
    diag = row == col
    causal = row >= col
    strict = row > col
    bdm = bdm_ref[...]
    bdm_f32 = bdm.astype(F32)

    def bd(m):
        mb = m.astype(BF16)
        return jnp.concatenate([mb, mb, mb, mb], axis=0) * bdm

    chains = [(c, bi, hf) for c in range(tc // CHUNK) for bi in range(nb) for hf in range(n_half)]

    pre = {}
    for ch in chains:
        c, bi, hf = ch
        rows = slice(c * CHUNK, (c + 1) * CHUNK)
        lanes = slice(hf * HALF, (hf + 1) * HALF)
        q = q_ref[bi, rows, lanes]
        k = k_ref[bi, rows, lanes]
        v = v_ref[bi, rows, lanes]
        bx = b_ref[bi, rows, lanes]
        gc = g_ref[bi, rows, lanes]
        for shift in (1, 2, 4, 8, 16, 32):
            gc = gc + jnp.where(row >= shift, pltpu.roll(gc, shift, 0), 0.0)
        gl = gc[CHUNK - 1:CHUNK, :]
        eg = jnp.exp(gc)
        kb = k * bx
        gc_t = jnp.sum(jnp.where(diag, gc, 0.0), axis=0, keepdims=True)
        decay = jnp.where(causal, jnp.exp(jnp.minimum(gc - gc_t, 0.0)), 0.0)
        s1 = _dot_nt(jnp.concatenate([k, q], axis=0).astype(BF16), bd(k))
        p0 = jnp.where(strict, -(s1[:CHUNK] * bx * decay), 0.0)
        pre[ch] = dict(
            p0=p0, qkd=(s1[CHUNK:] * decay).astype(BF16), vb=v * bx, w_in=kb * eg,
            qd=q * eg, kd=(k * jnp.exp(gl - gc)).astype(BF16), gle=jnp.exp(gl))

    tq = {}
    for ch in chains:
        p0 = pre[ch]["p0"]
        tq[ch] = (jnp.where(diag, 1.0, 0.0) + p0, _dot(p0.astype(BF16), bd(p0)))
    for lvl in range(1, 6):
        for ch in chains:
            t, qp = tq[ch]
            if lvl < 5:
                both = _dot(jnp.concatenate([qp, t], axis=0).astype(BF16), bd(qp))
                tq[ch] = (t + both[CHUNK:], both[:CHUNK])
            else:
                tq[ch] = (t + _dot(t.astype(BF16), bd(qp)), None)

    uw = {}
    for ch in chains:
        tb = tq[ch][0].astype(BF16)
        uw[ch] = (_dot(tb, bd(pre[ch]["vb"])), _dot(tb, bd(pre[ch]["w_in"])))

    for bi in range(nb):
        for hf in range(n_half):
            s = s_ref[bi * n_half + hf]
            for c in range(tc // CHUNK):
                ch = (c, bi, hf)
                u, w = uw[ch]
                pm = _dot(jnp.concatenate([w, pre[ch]["qd"]], axis=0).astype(BF16), s.astype(BF16))
                v_new = u - pm[:CHUNK]
                o = pm[CHUNK:] + _dot(pre[ch]["qkd"], bd(v_new))
                o_ref[bi, c * CHUNK:(c + 1) * CHUNK, hf * HALF:(hf + 1) * HALF] = o
                s = s * pre[ch]["gle"] + bdm_f32 * _dot_tn(pre[ch]["kd"], v_new.astype(BF16))
            s_ref[bi * n_half + hf] = s


def _gdn(q, k, v, beta_x, g_x, bdm, tc):
    b, t, _ = q.shape
    spec = pl.BlockSpec((b, tc, KEY_DIM), lambda ti: (0, ti, 0))
    return pl.pallas_call(
        _gdn_kernel,
        grid=(t // tc,),
        in_specs=[spec] * 5 + [_const_spec((HALF, HALF))],
        out_specs=spec,
        out_shape=jax.ShapeDtypeStruct((b, t, VAL_DIM), F32),
        scratch_shapes=[pltpu.VMEM((b * (KEY_DIM // HALF), HALF, HALF), F32)],
        compiler_params=pltpu.CompilerParams(
            dimension_semantics=("arbitrary",), vmem_limit_bytes=VMEM_LIMIT_BYTES),
        name="gdn",
    )(q, k, v, beta_x, g_x, bdm)


def _merge_kernel(x_ref, yc_ref, o_ref, szg_ref, sd_ref, ng_ref, wgo_ref, wo_ref, fg_ref,
                  wr_ref, br_ref, ones_ref, eye_ref,
                  xo_ref, eid_ref, wts_ref):
    tm = x_ref.shape[0]
    o = o_ref[...]
    ms = _dot((o * o).astype(BF16), ones_ref[...]) * (1.0 / HEAD_V)
    on = o * lax.rsqrt(ms + EPS) * ng_ref[...] * szg_ref[...].astype(F32)
    y_delta = _dot(on.astype(BF16), wgo_ref[...])
    y = yc_ref[...].astype(F32) + sd_ref[...].astype(F32) * y_delta
    x_new = x_ref[...] + _dot(y.astype(BF16), wo_ref[...])
    _rows_to_tiles(xo_ref, x_new)

    h2 = _rms_rows(x_new, fg_ref[...]).astype(BF16)
    logits = _dot_nt(wr_ref[...], h2) + br_ref[...]
    gl = logits[N_EXPERTS:N_EXPERTS + SUBLANES]
    grow = lax.broadcasted_iota(I32, (SUBLANES, tm), 0)
    gl = jnp.where(grow < N_GROUPS, gl, -jnp.inf)
    gmax = jnp.max(gl, axis=0, keepdims=True)
    group_p = 1.0 / jnp.sum(jnp.exp(gl - gmax), axis=0, keepdims=True)
    gi = jnp.min(jnp.where(gl == gmax, grow, SUBLANES), axis=0, keepdims=True)
    sel = jnp.zeros((EXPERTS_PER_GROUP, tm), F32)
    for g in range(N_GROUPS):
        sel = jnp.where(gi == g, logits[g * EXPERTS_PER_GROUP:(g + 1) * EXPERTS_PER_GROUP], sel)
    erow = lax.broadcasted_iota(I32, (EXPERTS_PER_GROUP, tm), 0)
    m1 = jnp.max(sel, axis=0, keepdims=True)
    i1 = jnp.min(jnp.where(sel == m1, erow, EXPERTS_PER_GROUP), axis=0, keepdims=True)
    sel2 = jnp.where(erow == i1, -jnp.inf, sel)
    m2 = jnp.max(sel2, axis=0, keepdims=True)
    i2 = jnp.min(jnp.where(sel2 == m2, erow, EXPERTS_PER_GROUP), axis=0, keepdims=True)
    e21 = jnp.exp(m2 - m1)
    w1 = group_p / (1.0 + e21)
    w2 = w1 * e21
    eid_ref[...] = jnp.concatenate(
        [gi * EXPERTS_PER_GROUP + i1, gi * EXPERTS_PER_GROUP + i2], axis=0)

    wrow = jnp.concatenate([w1, w2], axis=0)
    p1 = wrow.astype(BF16)
    r1 = wrow - p1.astype(F32)
    p2 = r1.astype(BF16)
    p3 = (r1 - p2.astype(F32)).astype(BF16)
    parts = jnp.concatenate([p1, p2, p3, jnp.zeros((2, tm), BF16)], axis=0)
    cols = _dot_nt(eye_ref[...], parts)
    wcol = cols[:, 0:2] + cols[:, 2:4] + cols[:, 4:6]
    wts_ref[...] = jnp.concatenate([wcol, jnp.zeros((tm, SUBLANES - 2), F32)], axis=1)


def _merge(x2, yc, o, szg, sd, ng_x, wgo, wo, fg, wr_t, br_t, ones_bd, eye, tm):
    n, d = x2.shape
    grid = (n // tm,)
    tile = lambda width: pl.BlockSpec((tm, width), lambda i: (i, 0))
    return pl.pallas_call(
        _merge_kernel,
        grid=grid,
        in_specs=[
            tile(d), tile(d), tile(VAL_DIM), tile(VAL_DIM), tile(d),
            _const_spec((1, VAL_DIM)),
            _const_spec((VAL_DIM, d)),
            _const_spec((d, d)),
            _const_spec((1, d)),
            _const_spec((ROUTER_ROWS, d)),
            _const_spec((ROUTER_ROWS, 1)),
            _const_spec((VAL_DIM, VAL_DIM)),
            _const_spec((tm, tm)),
        ],
        out_specs=(pl.BlockSpec((tm * SUBLANES, LANES), lambda i: (i, 0)),
                   pl.BlockSpec((2, tm), lambda i: (0, i)),
                   pl.BlockSpec((tm, SUBLANES), lambda i: (i, 0))),
        out_shape=(jax.ShapeDtypeStruct((n * SUBLANES, LANES), F32),
                   jax.ShapeDtypeStruct((2, n), I32),
                   jax.ShapeDtypeStruct((n, SUBLANES), F32)),
        compiler_params=pltpu.CompilerParams(
            dimension_semantics=("arbitrary",), vmem_limit_bytes=VMEM_LIMIT_BYTES),
        name="merge",
    )(x2, yc, o, szg, sd, ng_x, wgo, wo, fg, wr_t, br_t, ones_bd, eye)


def _plan_kernel(eid_ref, su_ref, lt_ref, pos_ref, te_ref, pad_ref, carry, offs, *, n_rows):
    tp = eid_ref.shape[1]
    ph = pl.program_id(0)
    i = pl.program_id(1)

    @pl.when(jnp.logical_and(ph == 0, i == 0))
    def _():
        carry[...] = jnp.zeros(carry.shape, F32)

    @pl.when(jnp.logical_and(ph == 1, i == 0))
    def _():
        count = carry[...]
        tiles = jnp.floor((count + (ROW_TILE - 1)) * (1.0 / ROW_TILE))
        hi = jnp.floor(tiles * (1.0 / SPLIT))
        lo = tiles - hi * SPLIT
        ex = _dot(lt_ref[...], jnp.concatenate([hi, lo], axis=1).astype(BF16))
        start = ex[:, :LANES] * SPLIT + ex[:, LANES:]
        offs[...] = start * ROW_TILE
        end = (start + tiles)[:, 0:1]
        tile_id = lax.broadcasted_iota(I32, (N_EXPERTS, te_ref.shape[1]), 1).astype(F32)
        te_ref[...] = jnp.sum(jnp.where(tile_id >= end, 1.0, 0.0), axis=0, keepdims=True).astype(I32)
        on_lane = (lax.broadcasted_iota(I32, (N_EXPERTS, LANES), 0)
                   == lax.broadcasted_iota(I32, (N_EXPERTS, LANES), 1))
        last = lax.broadcasted_iota(I32, (N_EXPERTS, LANES), 0) == N_EXPERTS - 1
        lo_rows = start * ROW_TILE + count
        hi_rows = jnp.where(last, float(n_rows), (start + tiles) * ROW_TILE)
        pad_ref[0:1, :] = jnp.sum(jnp.where(on_lane, lo_rows, 0.0), axis=0, keepdims=True).astype(I32)
        pad_ref[1:2, :] = jnp.sum(jnp.where(on_lane, hi_rows, 0.0), axis=0, keepdims=True).astype(I32)
        carry[...] = jnp.zeros(carry.shape, F32)

    e = eid_ref[...]
    rows = lax.broadcasted_iota(I32, (N_EXPERTS, tp), 0)
    hit0 = rows == e[0:1]
    hit1 = rows == e[1:2]
    hits = jnp.where(hit0 | hit1, 1.0, 0.0)

    @pl.when(ph == 1)
    def _():
        row = _dot(hits.astype(BF16), su_ref[...]) + (carry[:, 0:1] + offs[:, 0:1])
        r0 = jnp.sum(jnp.where(hit0, row, 0.0), axis=0, keepdims=True)
        r1 = jnp.sum(jnp.where(hit1, row, 0.0), axis=0, keepdims=True)
        pos_ref[...] = jnp.concatenate([r0, r1], axis=0).astype(I32)

    carry[...] = carry[...] + jnp.sum(hits, axis=1, keepdims=True)


def _plan(eid, su, lt, tp, te_pad, n_rows):
    n = eid.shape[1]
    return pl.pallas_call(
        functools.partial(_plan_kernel, n_rows=n_rows),
        grid=(2, n // tp),
        in_specs=[pl.BlockSpec((2, tp), lambda ph, i: (0, i)),
                  _const_spec((tp, tp)), _const_spec((N_EXPERTS, N_EXPERTS))],
        out_specs=(pl.BlockSpec((2, tp), lambda ph, i: (0, i * ph)),
                   pl.BlockSpec((1, te_pad), lambda ph, i: (0, 0)),
                   pl.BlockSpec((2, LANES), lambda ph, i: (0, 0))),
        out_shape=(jax.ShapeDtypeStruct((2, n), I32),
                   jax.ShapeDtypeStruct((1, te_pad), I32),
                   jax.ShapeDtypeStruct((2, LANES), I32)),
        scratch_shapes=[pltpu.VMEM((N_EXPERTS, LANES), F32), pltpu.VMEM((N_EXPERTS, LANES), F32)],
        compiler_params=pltpu.CompilerParams(dimension_semantics=("arbitrary", "arbitrary")),
        name="plan",
    )(eid, su, lt)


def _invert_kernel(pos_ref, pad_ref, inv_ref, *, n_tokens, tb):
    base = pl.program_id(0) * tb

    @pl.when(pl.program_id(0) == 0)
    def _():
        def clear_expert(e, carry):
            def clear(p, c):
                inv_ref[p] = 0
                return c
            return lax.fori_loop(pad_ref[e], pad_ref[LANES + e], clear, carry)
        lax.fori_loop(0, N_EXPERTS, clear_expert, 0)

    def scatter(t, carry):
        tok = base + t
        inv_ref[pos_ref[tok]] = tok
        inv_ref[pos_ref[n_tokens + tok]] = tok
        return carry
    lax.fori_loop(0, tb, scatter, 0, unroll=8)


def _invert(pos_flat, pad_flat, n_rows, tb):
    n = pos_flat.shape[0] // 2
    grid_spec = pltpu.PrefetchScalarGridSpec(
        num_scalar_prefetch=2,
        grid=(n // tb,),
        in_specs=[],
        out_specs=pl.BlockSpec(memory_space=pltpu.SMEM),
    )
    return pl.pallas_call(
        functools.partial(_invert_kernel, n_tokens=n, tb=tb),
        grid_spec=grid_spec,
        out_shape=jax.ShapeDtypeStruct((n_rows,), I32),
        compiler_params=pltpu.CompilerParams(dimension_semantics=("arbitrary",)),
        name="invert",
    )(pos_flat, pad_flat)


def _tile_dma(src_hbm, row, dst, r, sem):
    return pltpu.make_async_copy(
        src_hbm.at[pl.ds(pl.multiple_of(row * SUBLANES, SUBLANES), SUBLANES)],
        dst.at[pl.ds(pl.multiple_of(r * SUBLANES, SUBLANES), SUBLANES)], sem)


def _gather_tiles(n, row_of, src_hbm, dsts, sem):
    def block(blk, carry):
        for j in range(ISSUE_UNROLL):
            r = blk * ISSUE_UNROLL + j
            for k, dst in enumerate(dsts):
                _tile_dma(src_hbm, row_of(k, r), dst, r, sem).start(priority=(j + k) % 2)
        return carry
    lax.fori_loop(0, n // ISSUE_UNROLL, block, 0)


def _expert_kernel(te_ref, inv, x_hbm, fg_ref, wg_ref, wu_ref, wd_ref, ys_ref, xbuf, sems):
    i = pl.program_id(0)
    n_tiles = pl.num_programs(0)

    def gather(tile, slot):
        _gather_tiles(ROW_TILE, lambda k, r: inv[tile * ROW_TILE + r], x_hbm, [xbuf.at[slot]],
                      sems.at[slot])

    def gather_wait(slot):
        pltpu.make_async_copy(x_hbm.at[pl.ds(0, ROW_TILE * SUBLANES)], xbuf.at[slot],
                              sems.at[slot]).wait()

    @pl.when(jnp.logical_and(i == 0, te_ref[0] < N_EXPERTS))
    def _():
        gather(0, 0)

    nxt = jnp.minimum(i + 1, n_tiles - 1)

    @pl.when(jnp.logical_and(i + 1 < n_tiles, te_ref[nxt] < N_EXPERTS))
    def _():
        gather(i + 1, (i + 1) % 2)

    live = te_ref[i] < N_EXPERTS

    @pl.when(live)
    def _():
        slot = i % 2
        gather_wait(slot)
        x = _rows_from_tiles(xbuf.at[slot], ROW_TILE)
        h = _rms_rows(x, fg_ref[...]).astype(BF16)
        hid = _silu(_dot(h, wg_ref[0].astype(BF16))) * _dot(h, wu_ref[0].astype(BF16))
        _rows_to_tiles(ys_ref, _dot(hid.astype(BF16), wd_ref[0].astype(BF16)))

    @pl.when(jnp.logical_not(live))
    def _():
        ys_ref[...] = jnp.zeros(ys_ref.shape, F32)


def _experts(tile_expert, inv, x_tiles, fg, wg, wu, wd, layer):
    d = fg.shape[1]
    n_rows = inv.shape[0]
    expert = lambda i, te, inv: (layer, jnp.minimum(te[i], N_EXPERTS - 1), 0, 0)
    grid_spec = pltpu.PrefetchScalarGridSpec(
        num_scalar_prefetch=2,
        grid=(n_rows // ROW_TILE,),
        in_specs=[
            pl.BlockSpec(memory_space=pl.ANY),
            pl.BlockSpec((1, d), lambda i, te, inv: (0, 0)),
            pl.BlockSpec((None, 1, d, D_EXPERT), expert),
            pl.BlockSpec((None, 1, d, D_EXPERT), expert),
            pl.BlockSpec((None, 1, D_EXPERT, d), expert),
        ],
        out_specs=pl.BlockSpec((ROW_TILE * SUBLANES, LANES), lambda i, te, inv: (i, 0)),
        scratch_shapes=[
            pltpu.VMEM((2, ROW_TILE * SUBLANES, LANES), F32),
            pltpu.SemaphoreType.DMA((2,)),
        ],
    )
    return pl.pallas_call(
        _expert_kernel,
        grid_spec=grid_spec,
        out_shape=jax.ShapeDtypeStruct((n_rows * SUBLANES, LANES), F32),
        compiler_params=pltpu.CompilerParams(
            dimension_semantics=("arbitrary",), vmem_limit_bytes=VMEM_LIMIT_BYTES),
        name="experts",
    )(tile_expert, inv, x_tiles, fg, wg, wu, wd)


def _combine_kernel(pos_ref, x_ref, w_ref, fin_ref, ys_hbm, out_ref, ybuf, sems,
                    *, n_tokens, tc, final_norm):
    i = pl.program_id(0)
    n_steps = pl.num_programs(0)

    def gather(step, buf):
        _gather_tiles(tc, lambda slot, r: pos_ref[slot * n_tokens + step * tc + r], ys_hbm,
                      [ybuf.at[buf, 0], ybuf.at[buf, 1]], sems.at[buf])

    @pl.when(i == 0)
    def _():
        gather(0, 0)

    @pl.when(i + 1 < n_steps)
    def _():
        gather(i + 1, (i + 1) % 2)

    buf = i % 2
    for slot in range(2):
        pltpu.make_async_copy(ys_hbm.at[pl.ds(0, tc * SUBLANES)], ybuf.at[buf, slot],
                              sems.at[buf]).wait()

    w = w_ref[...]
    out = (_rows_from_tiles(x_ref, tc) + w[:, 0:1] * _rows_from_tiles(ybuf.at[buf, 0], tc)
           + w[:, 1:2] * _rows_from_tiles(ybuf.at[buf, 1], tc))
    if final_norm:
        out = _rms_rows(out, fin_ref[...])
    out_ref[...] = out


def _combine(pos_flat, x_tiles, wts, fin_gain, ys_tiles, tc, final_norm):
    n = wts.shape[0]
    d = fin_gain.shape[1]
    grid_spec = pltpu.PrefetchScalarGridSpec(
        num_scalar_prefetch=1,
        grid=(n // tc,),
        in_specs=[
            pl.BlockSpec((tc * SUBLANES, LANES), lambda i, pos: (i, 0)),
            pl.BlockSpec((tc, SUBLANES), lambda i, pos: (i, 0)),
            pl.BlockSpec((1, d), lambda i, pos: (0, 0)),
            pl.BlockSpec(memory_space=pl.ANY),
        ],
        out_specs=pl.BlockSpec((tc, d), lambda i, pos: (i, 0)),
        scratch_shapes=[pltpu.VMEM((2, 2, tc * SUBLANES, LANES), F32),
                        pltpu.SemaphoreType.DMA((2,))],
    )
    return pl.pallas_call(
        functools.partial(_combine_kernel, n_tokens=n, tc=tc, final_norm=final_norm),
        grid_spec=grid_spec,
        out_shape=jax.ShapeDtypeStruct((n, d), F32),
        compiler_params=pltpu.CompilerParams(
            dimension_semantics=("arbitrary",), vmem_limit_bytes=VMEM_LIMIT_BYTES),
        name="combine",
    )(pos_flat, x_tiles, wts, fin_gain, ys_tiles)


def _tile(n, target):
    t = min(n, target)
    while n % t:
        t //= 2
    return t


def _widen_w_in(w):
    o = 0
    conv = w[:, o:o + 2 * CONV_WIDTH]; o += 2 * CONV_WIDTH
    qkv = w[:, o:o + QKV_DIM]; o += QKV_DIM
    zg = w[:, o:o + VAL_DIM]; o += VAL_DIM
    wb = w[:, o:o + N_HEADS]; o += N_HEADS
    wa = w[:, o:o + N_HEADS]; o += N_HEADS
    gc = w[:, o:o + D_MODEL]; o += D_MODEL
    gd = w[:, o:o + D_MODEL]
    return jnp.concatenate(
        [conv, qkv, zg, jnp.repeat(wb, HEAD_K, axis=1), jnp.repeat(wa, HEAD_K, axis=1), gc, gd],
        axis=1).astype(BF16)


def kernel(x, norm_mix_gain, w_in, conv_dw_w, conv_dw_b, conv_ln_gain, conv_ln_bias, w_conv_out,
           gdn_conv_w, gdn_a_log, gdn_dt_bias, gdn_norm_gain, w_gdn_out, w_out,
           norm_ffn_gain, w_router_group, b_router_group, w_router_expert, b_router_expert,
           w_expert_gate, w_expert_up, w_expert_down, final_norm_gain):
    b, t, d = x.shape
    n = b * t
    depth = w_in.shape[0]
    tm = _tile(t, 512)
    tc_gdn = _tile(t, CHUNK)
    tm_merge = _tile(n, 512)
    tp = _tile(n, 512)
    t_inv = _tile(n, 2048)
    tc_comb = _tile(n, 256)
    n_rows = -(-(2 * n + N_EXPERTS * (ROW_TILE - 1)) // ROW_TILE) * ROW_TILE
    te_pad = -(-(n_rows // ROW_TILE) // LANES) * LANES

    head_of_lane = jnp.arange(KEY_DIM) // HEAD_K
    ones_bd = (head_of_lane[:, None] == head_of_lane[None, :]).astype(BF16)
    bdm = ones_bd[:HALF, :HALF]
    eye = jnp.eye(tm_merge, dtype=BF16)
    su = (jnp.arange(tp)[:, None] < jnp.arange(tp)[None, :]).astype(BF16)
    lt = (jnp.arange(N_EXPERTS)[:, None] > jnp.arange(N_EXPERTS)[None, :]).astype(BF16)
    row = lambda a: a.reshape(1, -1).astype(F32)

    for l in range(depth):
        yc, q, k, v, beta_x, g_x, szg, sd = _inproj(
            x, row(norm_mix_gain[l]), _widen_w_in(w_in[l]), conv_dw_w[l], row(conv_dw_b[l]),
            row(conv_ln_gain[l]), row(conv_ln_bias[l]), w_conv_out[l].astype(BF16), gdn_conv_w[l],
            row(jnp.repeat(gdn_a_log[l], HEAD_K)), row(jnp.repeat(gdn_dt_bias[l], HEAD_K)),
            ones_bd, tm)
        o = _gdn(q, k, v, beta_x, g_x, bdm, tc_gdn)

        wr_t = jnp.concatenate(
            [w_router_expert[l].T, w_router_group[l].T,
             jnp.zeros((ROUTER_ROWS - N_EXPERTS - N_GROUPS, d), F32)], axis=0).astype(BF16)
        br_t = jnp.concatenate(
            [b_router_expert[l], b_router_group[l],
             jnp.zeros((ROUTER_ROWS - N_EXPERTS - N_GROUPS,), F32)]).reshape(ROUTER_ROWS, 1)
        x_mid, eid, wts = _merge(
            x.reshape(n, d), yc.reshape(n, d), o.reshape(n, VAL_DIM), szg.reshape(n, VAL_DIM),
            sd.reshape(n, d), row(jnp.tile(gdn_norm_gain[l], N_HEADS)), w_gdn_out[l].astype(BF16),
            w_out[l].astype(BF16), row(norm_ffn_gain[l]), wr_t, br_t, ones_bd, eye, tm_merge)

        pos, tile_expert, pad_rows = _plan(eid, su, lt, tp, te_pad, n_rows)
        pos_flat = pos.reshape(2 * n)
        inv = _invert(pos_flat, pad_rows.reshape(2 * LANES), n_rows, t_inv)
        ys = _experts(tile_expert.reshape(te_pad), inv, x_mid, row(norm_ffn_gain[l]),
                      w_expert_gate, w_expert_up, w_expert_down, l)
        last = l == depth - 1
        x = _combine(pos_flat, x_mid, wts, row(final_norm_gain), ys, tc_comb, last
                     ).reshape(b, t, d)
    return x
```

```python
import functools

import jax
import jax.numpy as jnp
from jax import lax
from jax.experimental import pallas as pl
from jax.experimental.pallas import tpu as pltpu

F32 = jnp.float32
BF16 = jnp.bfloat16
I32 = jnp.int32

D_MODEL = 1024
CONV_WIDTH = 512
CONV_KERNEL = 31
N_HEADS = 8
HEAD_K = 64
HEAD_V = 64
KEY_DIM = N_HEADS * HEAD_K
VAL_DIM = N_HEADS * HEAD_V
QKV_DIM = 2 * KEY_DIM + VAL_DIM
SHORT_CONV = 4
CHUNK = 64
N_GROUPS = 4
EXPERTS_PER_GROUP = 8
N_EXPERTS = N_GROUPS * EXPERTS_PER_GROUP
D_EXPERT = 256
EPS = 1e-6

SUBLANES = 8
LANES = 128
VMEM_LIMIT_BYTES = 56 * 1024 * 1024

CONV_HALO = 32
QKV_HALO = SUBLANES
HALF = 4 * HEAD_K
ROW_TILE = 256
ROUTER_ROWS = 40
ISSUE_UNROLL = 8
SPLIT = 16

C_CONV = 0
C_QKV = C_CONV + 2 * CONV_WIDTH
C_ZG = C_QKV + QKV_DIM
C_BETA = C_ZG + VAL_DIM
C_DECAY = C_BETA + KEY_DIM
C_GATE_C = C_DECAY + KEY_DIM
C_GATE_D = C_GATE_C + D_MODEL
D_IN_WIDE = C_GATE_D + D_MODEL


def _dot(a, b):
    return jnp.dot(a, b, preferred_element_type=F32)


def _dot_nt(a, b):
    return lax.dot_general(a, b, (((1,), (1,)), ((), ())), preferred_element_type=F32)


def _dot_tn(a, b):
    return lax.dot_general(a, b, (((0,), (0,)), ((), ())), preferred_element_type=F32)


def _sigmoid(x):
    return 0.5 * jnp.tanh(0.5 * x) + 0.5


def _silu(x):
    return x * _sigmoid(x)


def _rms_rows(x, gain):
    return x * lax.rsqrt(jnp.mean(x * x, axis=-1, keepdims=True) + EPS) * gain


def _const_spec(shape):
    zeros = (0,) * len(shape)
    return pl.BlockSpec(shape, lambda *_: zeros, pipeline_mode=pl.Buffered(1))


def _rows_to_tiles(ref, val):
    rows = val.shape[0]
    for cb in range(val.shape[1] // LANES):
        ref[pl.ds(cb, rows, stride=SUBLANES), :] = val[:, cb * LANES:(cb + 1) * LANES]


def _rows_from_tiles(ref, rows):
    return jnp.concatenate(
        [ref[pl.ds(cb, rows, stride=SUBLANES), :] for cb in range(SUBLANES)], axis=1)


def _shift_up(a, r):
    return a if r == 0 else pltpu.roll(a, a.shape[0] - r, 0)


def _inproj_kernel(x_ref, gain_ref, w_ref, dww_ref, dwb_ref, lng_ref, lnb_ref, wco_ref,
                   scw_ref, alog_ref, dtb_ref, ones_ref,
                   yc_ref, q_ref, k_ref, v_ref, beta_ref, g_ref, szg_ref, sd_ref,
                   ubuf, qbuf):
    tm = x_ref.shape[1]
    ext = tm + SUBLANES

    @pl.when(pl.program_id(1) == 0)
    def _():
        ubuf[0:CONV_HALO, :] = jnp.zeros((CONV_HALO, CONV_WIDTH), F32)
        ubuf[CONV_HALO + tm:, :] = jnp.zeros((SUBLANES, CONV_WIDTH), F32)
        qbuf[0:QKV_HALO, :] = jnp.zeros((QKV_HALO, QKV_DIM), F32)
        qbuf[QKV_HALO + tm:, :] = jnp.zeros((SUBLANES, QKV_DIM), F32)

    h = _rms_rows(x_ref[0], gain_ref[...]).astype(BF16)

    c = _dot(h, w_ref[:, C_CONV:C_QKV])
    ubuf[CONV_HALO:CONV_HALO + tm, :] = c[:, :CONV_WIDTH] * _sigmoid(c[:, CONV_WIDTH:])
    first = CONV_HALO - (CONV_KERNEL - 1)
    acc = jnp.zeros((tm, CONV_WIDTH), F32) + dwb_ref[...]
    for r in range(SUBLANES):
        part = None
        for j in range(CONV_KERNEL):
            s = first + j
            if s % SUBLANES != r:
                continue
            term = ubuf[s - r:s - r + ext, :] * dww_ref[j:j + 1, :]
            part = term if part is None else part + term
        if part is not None:
            acc = acc + _shift_up(part, r)[0:tm]
    ubuf[0:CONV_HALO, :] = ubuf[tm:tm + CONV_HALO, :]
    mu = jnp.mean(acc, axis=-1, keepdims=True)
    xc = acc - mu
    ln = xc * lax.rsqrt(jnp.mean(xc * xc, axis=-1, keepdims=True) + EPS) * lng_ref[...] + lnb_ref[...]
    y_conv = _dot(_silu(ln).astype(BF16), wco_ref[...])
    gate_c = _sigmoid(_dot(h, w_ref[:, C_GATE_C:C_GATE_D]))
    yc_ref[0] = (gate_c * y_conv).astype(yc_ref.dtype)

    qbuf[QKV_HALO:QKV_HALO + tm, :] = _dot(h, w_ref[:, C_QKV:C_ZG])
    first = QKV_HALO - (SHORT_CONV - 1)
    qkv = jnp.zeros((tm, QKV_DIM), F32)
    for j in range(SHORT_CONV):
        s = first + j
        r = s % SUBLANES
        term = qbuf[s - r:s - r + ext, :] * scw_ref[j:j + 1, :]
        qkv = qkv + _shift_up(term, r)[0:tm]
    qbuf[0:QKV_HALO, :] = qbuf[tm:tm + QKV_HALO, :]
    qkv = _silu(qkv)
    q = qkv[:, :KEY_DIM]
    k = qkv[:, KEY_DIM:2 * KEY_DIM]
    ones = ones_ref[...]
    q_ref[0] = q * lax.rsqrt(_dot((q * q).astype(BF16), ones) + EPS) * (HEAD_K ** -0.5)
    k_ref[0] = k * lax.rsqrt(_dot((k * k).astype(BF16), ones) + EPS)
    v_ref[0] = qkv[:, 2 * KEY_DIM:]

    szg_ref[0] = _silu(_dot(h, w_ref[:, C_ZG:C_BETA])).astype(szg_ref.dtype)
    beta_ref[0] = _sigmoid(_dot(h, w_ref[:, C_BETA:C_DECAY]))
    a = _dot(h, w_ref[:, C_DECAY:C_GATE_C]) + dtb_ref[...]
    softplus = jnp.maximum(a, 0.0) + jnp.log1p(jnp.exp(-jnp.abs(a)))
    g_ref[0] = -jnp.exp(alog_ref[...]) * softplus
    sd_ref[0] = _sigmoid(_dot(h, w_ref[:, C_GATE_D:D_IN_WIDE])).astype(sd_ref.dtype)


def _inproj(x, gain, w_wide, dww, dwb, lng, lnb, wco, scw, alog_x, dtb_x, ones_bd, tm):
    b, t, d = x.shape
    grid = (b, t // tm)
    tile = lambda width: pl.BlockSpec((1, tm, width), lambda bi, ti: (bi, ti, 0))
    out_shape = (
        jax.ShapeDtypeStruct((b, t, D_MODEL), BF16),
        jax.ShapeDtypeStruct((b, t, KEY_DIM), F32),
        jax.ShapeDtypeStruct((b, t, KEY_DIM), F32),
        jax.ShapeDtypeStruct((b, t, VAL_DIM), F32),
        jax.ShapeDtypeStruct((b, t, KEY_DIM), F32),
        jax.ShapeDtypeStruct((b, t, KEY_DIM), F32),
        jax.ShapeDtypeStruct((b, t, VAL_DIM), BF16),
        jax.ShapeDtypeStruct((b, t, D_MODEL), BF16),
    )
    return pl.pallas_call(
        _inproj_kernel,
        grid=grid,
        in_specs=[
            tile(d),
            _const_spec((1, d)),
            _const_spec((d, D_IN_WIDE)),
            _const_spec((CONV_KERNEL, CONV_WIDTH)),
            _const_spec((1, CONV_WIDTH)),
            _const_spec((1, CONV_WIDTH)),
            _const_spec((1, CONV_WIDTH)),
            _const_spec((CONV_WIDTH, D_MODEL)),
            _const_spec((SHORT_CONV, QKV_DIM)),
            _const_spec((1, KEY_DIM)),
            _const_spec((1, KEY_DIM)),
            _const_spec((KEY_DIM, KEY_DIM)),
        ],
        out_specs=(tile(D_MODEL), tile(KEY_DIM), tile(KEY_DIM), tile(VAL_DIM), tile(KEY_DIM),
                   tile(KEY_DIM), tile(VAL_DIM), tile(D_MODEL)),
        out_shape=out_shape,
        scratch_shapes=[
            pltpu.VMEM((CONV_HALO + tm + SUBLANES, CONV_WIDTH), F32),
            pltpu.VMEM((QKV_HALO + tm + SUBLANES, QKV_DIM), F32),
        ],
        compiler_params=pltpu.CompilerParams(
            dimension_semantics=("arbitrary", "arbitrary"), vmem_limit_bytes=VMEM_LIMIT_BYTES),
        name="inproj",
    )(x, gain, w_wide, dww, dwb, lng, lnb, wco, scw, alog_x, dtb_x, ones_bd)


def _gdn_kernel(q_ref, k_ref, v_ref, b_ref, g_ref, bdm_ref, o_ref, s_ref):
    nb, tc = q_ref.shape[0], q_ref.shape[1]
    n_half = KEY_DIM // HALF

    @pl.when(pl.program_id(0) == 0)
    def _():
        s_ref[...] = jnp.zeros(s_ref.shape, F32)

    row = lax.broadcasted_iota(I32, (CHUNK, HALF), 0)
    col = lax.broadcasted_iota(I32, (CHUNK, HALF), 1) & (CHUNK - 1)
    diag = row == col
    causal = row >= col
    strict = row > col
    tri = (lax.broadcasted_iota(I32, (CHUNK, CHUNK), 0)
           >= lax.broadcasted_iota(I32, (CHUNK, CHUNK), 1)).astype(BF16)
    bdm = bdm_ref[...]
    bdm_f32 = bdm.astype(F32)

    def bd(m):
        mb = m.astype(BF16)
        return jnp.concatenate([mb, mb, mb, mb], axis=0) * bdm

    chains = [(c, bi, hf) for c in range(tc // CHUNK) for bi in range(nb) for hf in range(n_half)]

    pre = {}
    for ch in chains:
        c, bi, hf = ch
        rows = slice(c * CHUNK, (c + 1) * CHUNK)
        lanes = slice(hf * HALF, (hf + 1) * HALF)
        q = q_ref[bi, rows, lanes]
        k = k_ref[bi, rows, lanes]
        v = v_ref[bi, rows, lanes]
        bx = b_ref[bi, rows, lanes]
        gx = g_ref[bi, rows, lanes]
        g1 = gx.astype(BF16)
        r1 = gx - g1.astype(F32)
        g2 = r1.astype(BF16)
        g3 = (r1 - g2.astype(F32)).astype(BF16)
        gc3 = _dot(tri, jnp.concatenate([g1, g2, g3], axis=1))
        gc = gc3[:, :HALF] + gc3[:, HALF:2 * HALF] + gc3[:, 2 * HALF:]
        gl = gc[CHUNK - 1:CHUNK, :]
        eg = jnp.exp(gc)
        kb = k * bx
        gc_t = jnp.sum(jnp.where(diag, gc, 0.0), axis=0, keepdims=True)
        decay = jnp.where(causal, jnp.exp(jnp.minimum(gc - gc_t, 0.0)), 0.0)
        s1 = _dot_nt(jnp.concatenate([k, q], axis=0).astype(BF16), bd(k))
        p0 = jnp.where(strict, -(s1[:CHUNK] * bx * decay), 0.0)
        pre[ch] = dict(
            p0=p0, qkd=(s1[CHUNK:] * decay).astype(BF16), vb=v * bx, w_in=kb * eg,
            qd=q * eg, kd=(k * jnp.exp(gl - gc)).astype(BF16), gle=jnp.exp(gl))

    tq = {}
    for ch in chains:
        p0 = pre[ch]["p0"]
        tq[ch] = (jnp.where(diag, 1.0, 0.0) + p0, _dot(p0.astype(BF16), bd(p0)))
    for lvl in range(1, 6):
        for ch in chains:
            t, qp = tq[ch]
            if lvl < 5:
                both = _dot(jnp.concatenate([qp, t], axis=0).astype(BF16), bd(qp))
                tq[ch] = (t + both[CHUNK:], both[:CHUNK])
            else:
                tq[ch] = (t + _dot(t.astype(BF16), bd(qp)), None)

    uw = {}
    for ch in chains:
        tb = tq[ch][0].astype(BF16)
        uw[ch] = (_dot(tb, bd(pre[ch]["vb"])), _dot(tb, bd(pre[ch]["w_in"])))

    for bi in range(nb):
        for hf in range(n_half):
            s = s_ref[bi * n_half + hf]
            for c in range(tc // CHUNK):
                ch = (c, bi, hf)
                u, w = uw[ch]
                pm = _dot(jnp.concatenate([w, pre[ch]["qd"]], axis=0).astype(BF16), s.astype(BF16))
                v_new = u - pm[:CHUNK]
                o = pm[CHUNK:] + _dot(pre[ch]["qkd"], bd(v_new))
                o_ref[bi, c * CHUNK:(c + 1) * CHUNK, hf * HALF:(hf + 1) * HALF] = o
                s = s * pre[ch]["gle"] + bdm_f32 * _dot_tn(pre[ch]["kd"], v_new.astype(BF16))
            s_ref[bi * n_half + hf] = s


def _gdn(q, k, v, beta_x, g_x, bdm, tc):
    b, t, _ = q.shape
    spec = pl.BlockSpec((b, tc, KEY_DIM), lambda ti: (0, ti, 0))
    return pl.pallas_call(
        _gdn_kernel,
        grid=(t // tc,),
        in_specs=[spec] * 5 + [_const_spec((HALF, HALF))],
        out_specs=spec,
        out_shape=jax.ShapeDtypeStruct((b, t, VAL_DIM), F32),
        scratch_shapes=[pltpu.VMEM((b * (KEY_DIM // HALF), HALF, HALF), F32)],
        compiler_params=pltpu.CompilerParams(
            dimension_semantics=("arbitrary",), vmem_limit_bytes=VMEM_LIMIT_BYTES),
        name="gdn",
    )(q, k, v, beta_x, g_x, bdm)


def _merge_kernel(x_ref, yc_ref, o_ref, szg_ref, sd_ref, ng_ref, wgo_ref, wo_ref, fg_ref,
                  wr_ref, br_ref, ones_ref, eye_ref,
                  xo_ref, eid_ref, wts_ref):
    tm = x_ref.shape[0]
    o = o_ref[...]
    ms = _dot((o * o).astype(BF16), ones_ref[...]) * (1.0 / HEAD_V)
    on = o * lax.rsqrt(ms + EPS) * ng_ref[...] * szg_ref[...].astype(F32)
    y_delta = _dot(on.astype(BF16), wgo_ref[...])
    y = yc_ref[...].astype(F32) + sd_ref[...].astype(F32) * y_delta
    x_new = x_ref[...] + _dot(y.astype(BF16), wo_ref[...])
    _rows_to_tiles(xo_ref, x_new)

    h2 = _rms_rows(x_new, fg_ref[...]).astype(BF16)
    logits = _dot_nt(wr_ref[...], h2) + br_ref[...]
    gl = logits[N_EXPERTS:N_EXPERTS + SUBLANES]
    grow = lax.broadcasted_iota(I32, (SUBLANES, tm), 0)
    gl = jnp.where(grow < N_GROUPS, gl, -jnp.inf)
    gmax = jnp.max(gl, axis=0, keepdims=True)
    group_p = 1.0 / jnp.sum(jnp.exp(gl - gmax), axis=0, keepdims=True)
    gi = jnp.min(jnp.where(gl == gmax, grow, SUBLANES), axis=0, keepdims=True)
    sel = jnp.zeros((EXPERTS_PER_GROUP, tm), F32)
    for g in range(N_GROUPS):
        sel = jnp.where(gi == g, logits[g * EXPERTS_PER_GROUP:(g + 1) * EXPERTS_PER_GROUP], sel)
    erow = lax.broadcasted_iota(I32, (EXPERTS_PER_GROUP, tm), 0)
    m1 = jnp.max(sel, axis=0, keepdims=True)
    i1 = jnp.min(jnp.where(sel == m1, erow, EXPERTS_PER_GROUP), axis=0, keepdims=True)
    sel2 = jnp.where(erow == i1, -jnp.inf, sel)
    m2 = jnp.max(sel2, axis=0, keepdims=True)
    i2 = jnp.min(jnp.where(sel2 == m2, erow, EXPERTS_PER_GROUP), axis=0, keepdims=True)
    e21 = jnp.exp(m2 - m1)
    w1 = group_p / (1.0 + e21)
    w2 = w1 * e21
    eid_ref[...] = jnp.concatenate(
        [gi * EXPERTS_PER_GROUP + i1, gi * EXPERTS_PER_GROUP + i2], axis=0)

    wrow = jnp.concatenate([w1, w2], axis=0)
    p1 = wrow.astype(BF16)
    r1 = wrow - p1.astype(F32)
    p2 = r1.astype(BF16)
    p3 = (r1 - p2.astype(F32)).astype(BF16)
    parts = jnp.concatenate([p1, p2, p3, jnp.zeros((2, tm), BF16)], axis=0)
    cols = _dot_nt(eye_ref[...], parts)
    wcol = cols[:, 0:2] + cols[:, 2:4] + cols[:, 4:6]
    wts_ref[...] = jnp.concatenate([wcol, jnp.zeros((tm, SUBLANES - 2), F32)], axis=1)


def _merge(x2, yc, o, szg, sd, ng_x, wgo, wo, fg, wr_t, br_t, ones_bd, eye, tm):
    n, d = x2.shape
    grid = (n // tm,)
    tile = lambda width: pl.BlockSpec((tm, width), lambda i: (i, 0))
    return pl.pallas_call(
        _merge_kernel,
        grid=grid,
        in_specs=[
            tile(d), tile(d), tile(VAL_DIM), tile(VAL_DIM), tile(d),
            _const_spec((1, VAL_DIM)),
            _const_spec((VAL_DIM, d)),
            _const_spec((d, d)),
            _const_spec((1, d)),
            _const_spec((ROUTER_ROWS, d)),
            _const_spec((ROUTER_ROWS, 1)),
            _const_spec((VAL_DIM, VAL_DIM)),
            _const_spec((tm, tm)),
        ],
        out_specs=(pl.BlockSpec((tm * SUBLANES, LANES), lambda i: (i, 0)),
                   pl.BlockSpec((2, tm), lambda i: (0, i)),
                   pl.BlockSpec((tm, SUBLANES), lambda i: (i, 0))),
        out_shape=(jax.ShapeDtypeStruct((n * SUBLANES, LANES), F32),
                   jax.ShapeDtypeStruct((2, n), I32),
                   jax.ShapeDtypeStruct((n, SUBLANES), F32)),
        compiler_params=pltpu.CompilerParams(
            dimension_semantics=("arbitrary",), vmem_limit_bytes=VMEM_LIMIT_BYTES),
        name="merge",
    )(x2, yc, o, szg, sd, ng_x, wgo, wo, fg, wr_t, br_t, ones_bd, eye)


def _plan_kernel(eid_ref, su_ref, lt_ref, pos_ref, te_ref, pad_ref, carry, offs, *, n_rows):
    tp = eid_ref.shape[1]
    ph = pl.program_id(0)
    i = pl.program_id(1)

    @pl.when(jnp.logical_and(ph == 0, i == 0))
    def _():
        carry[...] = jnp.zeros(carry.shape, F32)

    @pl.when(jnp.logical_and(ph == 1, i == 0))
    def _():
        count = carry[...]
        tiles = jnp.floor((count + (ROW_TILE - 1)) * (1.0 / ROW_TILE))
        hi = jnp.floor(tiles * (1.0 / SPLIT))
        lo = tiles - hi * SPLIT
        ex = _dot(lt_ref[...], jnp.concatenate([hi, lo], axis=1).astype(BF16))
        start = ex[:, :LANES] * SPLIT + ex[:, LANES:]
        offs[...] = start * ROW_TILE
        end = (start + tiles)[:, 0:1]
        tile_id = lax.broadcasted_iota(I32, (N_EXPERTS, te_ref.shape[1]), 1).astype(F32)
        te_ref[...] = jnp.sum(jnp.where(tile_id >= end, 1.0, 0.0), axis=0, keepdims=True).astype(I32)
        on_lane = (lax.broadcasted_iota(I32, (N_EXPERTS, LANES), 0)
                   == lax.broadcasted_iota(I32, (N_EXPERTS, LANES), 1))
        last = lax.broadcasted_iota(I32, (N_EXPERTS, LANES), 0) == N_EXPERTS - 1
        lo_rows = start * ROW_TILE + count
        hi_rows = jnp.where(last, float(n_rows), (start + tiles) * ROW_TILE)
        pad_ref[0:1, :] = jnp.sum(jnp.where(on_lane, lo_rows, 0.0), axis=0, keepdims=True).astype(I32)
        pad_ref[1:2, :] = jnp.sum(jnp.where(on_lane, hi_rows, 0.0), axis=0, keepdims=True).astype(I32)
        carry[...] = jnp.zeros(carry.shape, F32)

    e = eid_ref[...]
    rows = lax.broadcasted_iota(I32, (N_EXPERTS, tp), 0)
    hit0 = rows == e[0:1]
    hit1 = rows == e[1:2]
    hits = jnp.where(hit0 | hit1, 1.0, 0.0)

    @pl.when(ph == 1)
    def _():
        row = _dot(hits.astype(BF16), su_ref[...]) + (carry[:, 0:1] + offs[:, 0:1])
        r0 = jnp.sum(jnp.where(hit0, row, 0.0), axis=0, keepdims=True)
        r1 = jnp.sum(jnp.where(hit1, row, 0.0), axis=0, keepdims=True)
        pos_ref[...] = jnp.concatenate([r0, r1], axis=0).astype(I32)

    carry[...] = carry[...] + jnp.sum(hits, axis=1, keepdims=True)


def _plan(eid, su, lt, tp, te_pad, n_rows):
    n = eid.shape[1]
    return pl.pallas_call(
        functools.partial(_plan_kernel, n_rows=n_rows),
        grid=(2, n // tp),
        in_specs=[pl.BlockSpec((2, tp), lambda ph, i: (0, i)),
                  _const_spec((tp, tp)), _const_spec((N_EXPERTS, N_EXPERTS))],
        out_specs=(pl.BlockSpec((2, tp), lambda ph, i: (0, i * ph)),
                   pl.BlockSpec((1, te_pad), lambda ph, i: (0, 0)),
                   pl.BlockSpec((2, LANES), lambda ph, i: (0, 0))),
        out_shape=(jax.ShapeDtypeStruct((2, n), I32),
                   jax.ShapeDtypeStruct((1, te_pad), I32),
                   jax.ShapeDtypeStruct((2, LANES), I32)),
        scratch_shapes=[pltpu.VMEM((N_EXPERTS, LANES), F32), pltpu.VMEM((N_EXPERTS, LANES), F32)],
        compiler_params=pltpu.CompilerParams(dimension_semantics=("arbitrary", "arbitrary")),
        name="plan",
    )(eid, su, lt)


def _invert_kernel(pos_ref, pad_ref, inv_ref, *, n_tokens, tb):
    base = pl.program_id(0) * tb

    @pl.when(pl.program_id(0) == 0)
    def _():
        def clear_expert(e, carry):
            def clear(p, c):
                inv_ref[p] = 0
                return c
            return lax.fori_loop(pad_ref[e], pad_ref[LANES + e], clear, carry)
        lax.fori_loop(0, N_EXPERTS, clear_expert, 0)

    def scatter(t, carry):
        tok = base + t
        inv_ref[pos_ref[tok]] = tok
        inv_ref[pos_ref[n_tokens + tok]] = tok
        return carry
    lax.fori_loop(0, tb, scatter, 0, unroll=8)


def _invert(pos_flat, pad_flat, n_rows, tb):
    n = pos_flat.shape[0] // 2
    grid_spec = pltpu.PrefetchScalarGridSpec(
        num_scalar_prefetch=2,
        grid=(n // tb,),
        in_specs=[],
        out_specs=pl.BlockSpec(memory_space=pltpu.SMEM),
    )
    return pl.pallas_call(
        functools.partial(_invert_kernel, n_tokens=n, tb=tb),
        grid_spec=grid_spec,
        out_shape=jax.ShapeDtypeStruct((n_rows,), I32),
        compiler_params=pltpu.CompilerParams(dimension_semantics=("arbitrary",)),
        name="invert",
    )(pos_flat, pad_flat)


def _tile_dma(src_hbm, row, dst, r, sem):
    dst_rows = r * SUBLANES if isinstance(r, int) else pl.multiple_of(r * SUBLANES, SUBLANES)
    return pltpu.make_async_copy(
        src_hbm.at[pl.ds(pl.multiple_of(row * SUBLANES, SUBLANES), SUBLANES)],
        dst.at[pl.ds(dst_rows, SUBLANES)], sem)


def _gather_tiles(n, row_of, src_hbm, dsts, sem, inline):
    def start(r, j):
        for k, dst in enumerate(dsts):
            _tile_dma(src_hbm, row_of(k, r), dst, r, sem).start(priority=(j + k) % 2)

    if inline:
        for r in range(n):
            start(r, r)
    else:
        def block(blk, carry):
            for j in range(ISSUE_UNROLL):
                start(blk * ISSUE_UNROLL + j, j)
            return carry
        lax.fori_loop(0, n // ISSUE_UNROLL, block, 0)


def _expert_kernel(te_ref, inv, x_hbm, fg_ref, wg_ref, wu_ref, wd_ref, ys_ref, xbuf0, xbuf1, sems,
                   *, n_tiles):
    i = pl.program_id(0)
    live = te_ref[i] < N_EXPERTS
    prev_live = jnp.logical_and(i > 0, te_ref[jnp.maximum(i - 1, 0)] < N_EXPERTS)
    has_next = i + 1 < n_tiles
    bufs = (xbuf0, xbuf1)

    def gather(tile, p, inline):
        _gather_tiles(ROW_TILE, lambda k, r: inv[tile * ROW_TILE + r], x_hbm, [bufs[p]],
                      sems.at[p], inline)

    def gather_wait(p):
        pltpu.make_async_copy(x_hbm.at[pl.ds(0, ROW_TILE * SUBLANES)], bufs[p], sems.at[p]).wait()

    def compute(p):
        x = _rows_from_tiles(bufs[p], ROW_TILE)
        h = _rms_rows(x, fg_ref[...]).astype(BF16)
        hid = _silu(_dot(h, wg_ref[0].astype(BF16))) * _dot(h, wu_ref[0].astype(BF16))
        _rows_to_tiles(ys_ref, _dot(hid.astype(BF16), wd_ref[0].astype(BF16)))

    @pl.when(jnp.logical_and(i == 0, live))
    def _():
        gather(0, 0, False)

    def parity_step(p):
        mine = (i % 2) == p

        @pl.when(mine & live & has_next)
        def _():
            gather_wait(p)
            gather(i + 1, 1 - p, True)
            compute(p)

        if (n_tiles - 1) % 2 == p:
            @pl.when(mine & live & jnp.logical_not(has_next))
            def _():
                gather_wait(p)
                compute(p)

        @pl.when(mine & jnp.logical_not(live) & prev_live)
        def _():
            gather_wait(p)

    parity_step(0)
    parity_step(1)

    @pl.when(jnp.logical_not(live))
    def _():
        ys_ref[...] = jnp.zeros(ys_ref.shape, F32)


def _experts(tile_expert, inv, x_tiles, fg, wg, wu, wd, layer):
    d = fg.shape[1]
    n_rows = inv.shape[0]
    expert = lambda i, te, inv: (layer, jnp.minimum(te[i], N_EXPERTS - 1), 0, 0)
    grid_spec = pltpu.PrefetchScalarGridSpec(
        num_scalar_prefetch=2,
        grid=(n_rows // ROW_TILE,),
        in_specs=[
            pl.BlockSpec(memory_space=pl.ANY),
            pl.BlockSpec((1, d), lambda i, te, inv: (0, 0)),
            pl.BlockSpec((None, 1, d, D_EXPERT), expert),
            pl.BlockSpec((None, 1, d, D_EXPERT), expert),
            pl.BlockSpec((None, 1, D_EXPERT, d), expert),
        ],
        out_specs=pl.BlockSpec((ROW_TILE * SUBLANES, LANES), lambda i, te, inv: (i, 0)),
        scratch_shapes=[
            pltpu.VMEM((ROW_TILE * SUBLANES, LANES), F32),
            pltpu.VMEM((ROW_TILE * SUBLANES, LANES), F32),
            pltpu.SemaphoreType.DMA((2,)),
        ],
    )
    return pl.pallas_call(
        functools.partial(_expert_kernel, n_tiles=n_rows // ROW_TILE),
        grid_spec=grid_spec,
        out_shape=jax.ShapeDtypeStruct((n_rows * SUBLANES, LANES), F32),
        compiler_params=pltpu.CompilerParams(
            dimension_semantics=("arbitrary",), vmem_limit_bytes=VMEM_LIMIT_BYTES),
        name="experts",
    )(tile_expert, inv, x_tiles, fg, wg, wu, wd)


def _combine_kernel(pos_ref, x_ref, w_ref, fin_ref, ys_hbm, out_ref, ybuf0, ybuf1, sems,
                    *, n_tokens, tc, n_steps, final_norm):
    i = pl.program_id(0)
    bufs = (ybuf0, ybuf1)

    def gather(step, p, inline):
        _gather_tiles(tc, lambda slot, r: pos_ref[slot * n_tokens + step * tc + r], ys_hbm,
                      [bufs[p].at[0], bufs[p].at[1]], sems.at[p], inline)

    def gather_wait(p):
        for slot in range(2):
            pltpu.make_async_copy(ys_hbm.at[pl.ds(0, tc * SUBLANES)], bufs[p].at[slot],
                                  sems.at[p]).wait()

    def compute(p):
        w = w_ref[...]
        out = (_rows_from_tiles(x_ref, tc) + w[:, 0:1] * _rows_from_tiles(bufs[p].at[0], tc)
               + w[:, 1:2] * _rows_from_tiles(bufs[p].at[1], tc))
        if final_norm:
            out = _rms_rows(out, fin_ref[...])
        out_ref[...] = out

    @pl.when(i == 0)
    def _():
        gather(0, 0, False)

    def parity_step(p):
        mine = (i % 2) == p

        @pl.when(mine & (i + 1 < n_steps))
        def _():
            gather_wait(p)
            gather(i + 1, 1 - p, True)
            compute(p)

        if (n_steps - 1) % 2 == p:
            @pl.when(mine & (i + 1 == n_steps))
            def _():
                gather_wait(p)
                compute(p)

    parity_step(0)
    parity_step(1)


def _combine(pos_flat, x_tiles, wts, fin_gain, ys_tiles, tc, final_norm):
    n = wts.shape[0]
    d = fin_gain.shape[1]
    grid_spec = pltpu.PrefetchScalarGridSpec(
        num_scalar_prefetch=1,
        grid=(n // tc,),
        in_specs=[
            pl.BlockSpec((tc * SUBLANES, LANES), lambda i, pos: (i, 0)),
            pl.BlockSpec((tc, SUBLANES), lambda i, pos: (i, 0)),
            pl.BlockSpec((1, d), lambda i, pos: (0, 0)),
            pl.BlockSpec(memory_space=pl.ANY),
        ],
        out_specs=pl.BlockSpec((tc, d), lambda i, pos: (i, 0)),
        scratch_shapes=[pltpu.VMEM((2, tc * SUBLANES, LANES), F32),
                        pltpu.VMEM((2, tc * SUBLANES, LANES), F32),
                        pltpu.SemaphoreType.DMA((2,))],
    )
    return pl.pallas_call(
        functools.partial(_combine_kernel, n_tokens=n, tc=tc, n_steps=n // tc,
                          final_norm=final_norm),
        grid_spec=grid_spec,
        out_shape=jax.ShapeDtypeStruct((n, d), F32),
        compiler_params=pltpu.CompilerParams(
            dimension_semantics=("arbitrary",), vmem_limit_bytes=VMEM_LIMIT_BYTES),
        name="combine",
    )(pos_flat, x_tiles, wts, fin_gain, ys_tiles)


def _tile(n, target):
    t = min(n, target)
    while n % t:
        t //= 2
    return t


def _widen_w_in(w):
    o = 0
    conv = w[:, o:o + 2 * CONV_WIDTH]; o += 2 * CONV_WIDTH
    qkv = w[:, o:o + QKV_DIM]; o += QKV_DIM
    zg = w[:, o:o + VAL_DIM]; o += VAL_DIM
    wb = w[:, o:o + N_HEADS]; o += N_HEADS
    wa = w[:, o:o + N_HEADS]; o += N_HEADS
    gc = w[:, o:o + D_MODEL]; o += D_MODEL
    gd = w[:, o:o + D_MODEL]
    return jnp.concatenate(
        [conv, qkv, zg, jnp.repeat(wb, HEAD_K, axis=1), jnp.repeat(wa, HEAD_K, axis=1), gc, gd],
        axis=1).astype(BF16)


def kernel(x, norm_mix_gain, w_in, conv_dw_w, conv_dw_b, conv_ln_gain, conv_ln_bias, w_conv_out,
           gdn_conv_w, gdn_a_log, gdn_dt_bias, gdn_norm_gain, w_gdn_out, w_out,
           norm_ffn_gain, w_router_group, b_router_group, w_router_expert, b_router_expert,
           w_expert_gate, w_expert_up, w_expert_down, final_norm_gain):
    b, t, d = x.shape
    n = b * t
    depth = w_in.shape[0]
    tm = _tile(t, 512)
    tc_gdn = _tile(t, CHUNK)
    tm_merge = _tile(n, 512)
    tp = _tile(n, 512)
    t_inv = _tile(n, 2048)
    tc_comb = _tile(n, 256)
    n_rows = -(-(2 * n + N_EXPERTS * (ROW_TILE - 1)) // ROW_TILE) * ROW_TILE
    te_pad = -(-(n_rows // ROW_TILE) // LANES) * LANES

    head_of_lane = jnp.arange(KEY_DIM) // HEAD_K
    ones_bd = (head_of_lane[:, None] == head_of_lane[None, :]).astype(BF16)
    bdm = ones_bd[:HALF, :HALF]
    eye = jnp.eye(tm_merge, dtype=BF16)
    su = (jnp.arange(tp)[:, None] < jnp.arange(tp)[None, :]).astype(BF16)
    lt = (jnp.arange(N_EXPERTS)[:, None] > jnp.arange(N_EXPERTS)[None, :]).astype(BF16)
    row = lambda a: a.reshape(1, -1).astype(F32)

    for l in range(depth):
        yc, q, k, v, beta_x, g_x, szg, sd = _inproj(
            x, row(norm_mix_gain[l]), _widen_w_in(w_in[l]), conv_dw_w[l], row(conv_dw_b[l]),
            row(conv_ln_gain[l]), row(conv_ln_bias[l]), w_conv_out[l].astype(BF16), gdn_conv_w[l],
            row(jnp.repeat(gdn_a_log[l], HEAD_K)), row(jnp.repeat(gdn_dt_bias[l], HEAD_K)),
            ones_bd, tm)
        o = _gdn(q, k, v, beta_x, g_x, bdm, tc_gdn)

        wr_t = jnp.concatenate(
            [w_router_expert[l].T, w_router_group[l].T,
             jnp.zeros((ROUTER_ROWS - N_EXPERTS - N_GROUPS, d), F32)], axis=0).astype(BF16)
        br_t = jnp.concatenate(
            [b_router_expert[l], b_router_group[l],
             jnp.zeros((ROUTER_ROWS - N_EXPERTS - N_GROUPS,), F32)]).reshape(ROUTER_ROWS, 1)
        x_mid, eid, wts = _merge(
            x.reshape(n, d), yc.reshape(n, d), o.reshape(n, VAL_DIM), szg.reshape(n, VAL_DIM),
            sd.reshape(n, d), row(jnp.tile(gdn_norm_gain[l], N_HEADS)), w_gdn_out[l].astype(BF16),
            w_out[l].astype(BF16), row(norm_ffn_gain[l]), wr_t, br_t, ones_bd, eye, tm_merge)

        pos, tile_expert, pad_rows = _plan(eid, su, lt, tp, te_pad, n_rows)
        pos_flat = pos.reshape(2 * n)
        inv = _invert(pos_flat, pad_rows.reshape(2 * LANES), n_rows, t_inv)
        ys = _experts(tile_expert.reshape(te_pad), inv, x_mid, row(norm_ffn_gain[l]),
                      w_expert_gate, w_expert_up, w_expert_down, l)
        last = l == depth - 1
        x = _combine(pos_flat, x_mid, wts, row(final_norm_gain), ys, tc_comb, last
                     ).reshape(b, t, d)
    return x
```

```python
import functools

import jax
import jax.numpy as jnp
from jax import lax
from jax.experimental import pallas as pl
from jax.experimental.pallas import tpu as pltpu

F32 = jnp.float32
BF16 = jnp.bfloat16
I32 = jnp.int32

D_MODEL = 1024
CONV_WIDTH = 512
CONV_KERNEL = 31
N_HEADS = 8
HEAD_K = 64
HEAD_V = 64
KEY_DIM = N_HEADS * HEAD_K
VAL_DIM = N_HEADS * HEAD_V
QKV_DIM = 2 * KEY_DIM + VAL_DIM
SHORT_CONV = 4
CHUNK = 64
N_GROUPS = 4
EXPERTS_PER_GROUP = 8
N_EXPERTS = N_GROUPS * EXPERTS_PER_GROUP
D_EXPERT = 256
EPS = 1e-6

SUBLANES = 8
LANES = 128
VMEM_LIMIT_BYTES = 56 * 1024 * 1024

CONV_HALO = 32
QKV_HALO = SUBLANES
HALF = 4 * HEAD_K
ROW_TILE = 256
ROUTER_ROWS = 40
ISSUE_UNROLL = 8
GATHER_AHEAD = 2
N_XBUF = GATHER_AHEAD + 1
SPLIT = 16

C_CONV = 0
C_QKV = C_CONV + 2 * CONV_WIDTH
C_ZG = C_QKV + QKV_DIM
C_BETA = C_ZG + VAL_DIM
C_DECAY = C_BETA + KEY_DIM
C_GATE_C = C_DECAY + KEY_DIM
C_GATE_D = C_GATE_C + D_MODEL
D_IN_WIDE = C_GATE_D + D_MODEL


def _dot(a, b):
    return jnp.dot(a, b, preferred_element_type=F32)


def _dot_nt(a, b):
    return lax.dot_general(a, b, (((1,), (1,)), ((), ())), preferred_element_type=F32)


def _dot_tn(a, b):
    return lax.dot_general(a, b, (((0,), (0,)), ((), ())), preferred_element_type=F32)


def _sigmoid(x):
    return 0.5 * jnp.tanh(0.5 * x) + 0.5


def _silu(x):
    return x * _sigmoid(x)


def _rms_rows(x, gain):
    return x * lax.rsqrt(jnp.mean(x * x, axis=-1, keepdims=True) + EPS) * gain


def _const_spec(shape):
    zeros = (0,) * len(shape)
    return pl.BlockSpec(shape, lambda *_: zeros, pipeline_mode=pl.Buffered(1))


def _rows_to_tiles(ref, val):
    rows = val.shape[0]
    for cb in range(val.shape[1] // LANES):
        ref[pl.ds(cb, rows, stride=SUBLANES), :] = val[:, cb * LANES:(cb + 1) * LANES]


def _rows_from_tiles(ref, rows):
    return jnp.concatenate(
        [ref[pl.ds(cb, rows, stride=SUBLANES), :] for cb in range(SUBLANES)], axis=1)


def _shift_up(a, r):
    return a if r == 0 else pltpu.roll(a, a.shape[0] - r, 0)


def _inproj_kernel(x_ref, gain_ref, w_ref, dww_ref, dwb_ref, lng_ref, lnb_ref, wco_ref,
                   scw_ref, alog_ref, dtb_ref, ones_ref,
                   yc_ref, q_ref, k_ref, v_ref, beta_ref, g_ref, szg_ref, sd_ref,
                   ubuf, qbuf):
    tm = x_ref.shape[1]
    ext = tm + SUBLANES

    @pl.when(pl.program_id(1) == 0)
    def _():
        ubuf[0:CONV_HALO, :] = jnp.zeros((CONV_HALO, CONV_WIDTH), F32)
        ubuf[CONV_HALO + tm:, :] = jnp.zeros((SUBLANES, CONV_WIDTH), F32)
        qbuf[0:QKV_HALO, :] = jnp.zeros((QKV_HALO, QKV_DIM), F32)
        qbuf[QKV_HALO + tm:, :] = jnp.zeros((SUBLANES, QKV_DIM), F32)

    h = _rms_rows(x_ref[0], gain_ref[...]).astype(BF16)

    c = _dot(h, w_ref[:, C_CONV:C_QKV])
    ubuf[CONV_HALO:CONV_HALO + tm, :] = c[:, :CONV_WIDTH] * _sigmoid(c[:, CONV_WIDTH:])
    first = CONV_HALO - (CONV_KERNEL - 1)
    acc = jnp.zeros((tm, CONV_WIDTH), F32) + dwb_ref[...]
    for r in range(SUBLANES):
        part = None
        for j in range(CONV_KERNEL):
            s = first + j
            if s % SUBLANES != r:
                continue
            term = ubuf[s - r:s - r + ext, :] * dww_ref[j:j + 1, :]
            part = term if part is None else part + term
        if part is not None:
            acc = acc + _shift_up(part, r)[0:tm]
    ubuf[0:CONV_HALO, :] = ubuf[tm:tm + CONV_HALO, :]
    mu = jnp.mean(acc, axis=-1, keepdims=True)
    xc = acc - mu
    ln = xc * lax.rsqrt(jnp.mean(xc * xc, axis=-1, keepdims=True) + EPS) * lng_ref[...] + lnb_ref[...]
    y_conv = _dot(_silu(ln).astype(BF16), wco_ref[...])
    gate_c = _sigmoid(_dot(h, w_ref[:, C_GATE_C:C_GATE_D]))
    yc_ref[0] = (gate_c * y_conv).astype(yc_ref.dtype)

    qbuf[QKV_HALO:QKV_HALO + tm, :] = _dot(h, w_ref[:, C_QKV:C_ZG])
    first = QKV_HALO - (SHORT_CONV - 1)
    qkv = jnp.zeros((tm, QKV_DIM), F32)
    for j in range(SHORT_CONV):
        s = first + j
        r = s % SUBLANES
        term = qbuf[s - r:s - r + ext, :] * scw_ref[j:j + 1, :]
        qkv = qkv + _shift_up(term, r)[0:tm]
    qbuf[0:QKV_HALO, :] = qbuf[tm:tm + QKV_HALO, :]
    qkv = _silu(qkv)
    q = qkv[:, :KEY_DIM]
    k = qkv[:, KEY_DIM:2 * KEY_DIM]
    ones = ones_ref[...]
    q_ref[0] = q * lax.rsqrt(_dot((q * q).astype(BF16), ones) + EPS) * (HEAD_K ** -0.5)
    k_ref[0] = k * lax.rsqrt(_dot((k * k).astype(BF16), ones) + EPS)
    v_ref[0] = qkv[:, 2 * KEY_DIM:]

    szg_ref[0] = _silu(_dot(h, w_ref[:, C_ZG:C_BETA])).astype(szg_ref.dtype)
    beta_ref[0] = _sigmoid(_dot(h, w_ref[:, C_BETA:C_DECAY]))
    a = _dot(h, w_ref[:, C_DECAY:C_GATE_C]) + dtb_ref[...]
    softplus = jnp.maximum(a, 0.0) + jnp.log1p(jnp.exp(-jnp.abs(a)))
    g_ref[0] = -jnp.exp(alog_ref[...]) * softplus
    sd_ref[0] = _sigmoid(_dot(h, w_ref[:, C_GATE_D:D_IN_WIDE])).astype(sd_ref.dtype)


def _inproj(x, gain, w_wide, dww, dwb, lng, lnb, wco, scw, alog_x, dtb_x, ones_bd, tm):
    b, t, d = x.shape
    grid = (b, t // tm)
    tile = lambda width: pl.BlockSpec((1, tm, width), lambda bi, ti: (bi, ti, 0))
    out_shape = (
        jax.ShapeDtypeStruct((b, t, D_MODEL), BF16),
        jax.ShapeDtypeStruct((b, t, KEY_DIM), F32),
        jax.ShapeDtypeStruct((b, t, KEY_DIM), F32),
        jax.ShapeDtypeStruct((b, t, VAL_DIM), F32),
        jax.ShapeDtypeStruct((b, t, KEY_DIM), F32),
        jax.ShapeDtypeStruct((b, t, KEY_DIM), F32),
        jax.ShapeDtypeStruct((b, t, VAL_DIM), BF16),
        jax.ShapeDtypeStruct((b, t, D_MODEL), BF16),
    )
    return pl.pallas_call(
        _inproj_kernel,
        grid=grid,
        in_specs=[
            tile(d),
            _const_spec((1, d)),
            _const_spec((d, D_IN_WIDE)),
            _const_spec((CONV_KERNEL, CONV_WIDTH)),
            _const_spec((1, CONV_WIDTH)),
            _const_spec((1, CONV_WIDTH)),
            _const_spec((1, CONV_WIDTH)),
            _const_spec((CONV_WIDTH, D_MODEL)),
            _const_spec((SHORT_CONV, QKV_DIM)),
            _const_spec((1, KEY_DIM)),
            _const_spec((1, KEY_DIM)),
            _const_spec((KEY_DIM, KEY_DIM)),
        ],
        out_specs=(tile(D_MODEL), tile(KEY_DIM), tile(KEY_DIM), tile(VAL_DIM), tile(KEY_DIM),
                   tile(KEY_DIM), tile(VAL_DIM), tile(D_MODEL)),
        out_shape=out_shape,
        scratch_shapes=[
            pltpu.VMEM((CONV_HALO + tm + SUBLANES, CONV_WIDTH), F32),
            pltpu.VMEM((QKV_HALO + tm + SUBLANES, QKV_DIM), F32),
        ],
        compiler_params=pltpu.CompilerParams(
            dimension_semantics=("arbitrary", "arbitrary"), vmem_limit_bytes=VMEM_LIMIT_BYTES),
        name="inproj",
    )(x, gain, w_wide, dww, dwb, lng, lnb, wco, scw, alog_x, dtb_x, ones_bd)


def _gdn_kernel(q_ref, k_ref, v_ref, b_ref, g_ref, bdm_ref, o_ref, s_ref):
    nb, tc = q_ref.shape[0], q_ref.shape[1]
    n_half = KEY_DIM // HALF

    @pl.when(pl.program_id(0) == 0)
    def _():
        s_ref[...] = jnp.zeros(s_ref.shape, F32)

    row = lax.broadcasted_iota(I32, (CHUNK, HALF), 0)
    col = lax.broadcasted_iota(I32, (CHUNK, HALF), 1) & (CHUNK - 1)
    diag = row == col
    causal = row >= col
    strict = row > col
    tri = (lax.broadcasted_iota(I32, (CHUNK, CHUNK), 0)
           >= lax.broadcasted_iota(I32, (CHUNK, CHUNK), 1)).astype(BF16)
    bdm = bdm_ref[...]
    bdm_f32 = bdm.astype(F32)

    def bd(m):
        mb = m.astype(BF16)
        return jnp.concatenate([mb, mb, mb, mb], axis=0) * bdm

    chains = [(c, bi, hf) for c in range(tc // CHUNK) for bi in range(nb) for hf in range(n_half)]

    pre = {}
    for ch in chains:
        c, bi, hf = ch
        rows = slice(c * CHUNK, (c + 1) * CHUNK)
        lanes = slice(hf * HALF, (hf + 1) * HALF)
        q = q_ref[bi, rows, lanes]
        k = k_ref[bi, rows, lanes]
        v = v_ref[bi, rows, lanes]
        bx = b_ref[bi, rows, lanes]
        gx = g_ref[bi, rows, lanes]
        g1 = gx.astype(BF16)
        r1 = gx - g1.astype(F32)
        g2 = r1.astype(BF16)
        g3 = (r1 - g2.astype(F32)).astype(BF16)
        gc3 = _dot(tri, jnp.concatenate([g1, g2, g3], axis=1))
        gc = gc3[:, :HALF] + gc3[:, HALF:2 * HALF] + gc3[:, 2 * HALF:]
        gl = gc[CHUNK - 1:CHUNK, :]
        eg = jnp.exp(gc)
        kb = k * bx
        gc_t = jnp.sum(jnp.where(diag, gc, 0.0), axis=0, keepdims=True)
        decay = jnp.where(causal, jnp.exp(jnp.minimum(gc - gc_t, 0.0)), 0.0)
        s1 = _dot_nt(jnp.concatenate([k, q], axis=0).astype(BF16), bd(k))
        p0 = jnp.where(strict, -(s1[:CHUNK] * bx * decay), 0.0)
        pre[ch] = dict(
            p0=p0, qkd=(s1[CHUNK:] * decay).astype(BF16), vb=v * bx, w_in=kb * eg,
            qd=q * eg, kd=(k * jnp.exp(gl - gc)).astype(BF16), gle=jnp.exp(gl))

    tq = {}
    for ch in chains:
        p0 = pre[ch]["p0"]
        tq[ch] = (jnp.where(diag, 1.0, 0.0) + p0, _dot(p0.astype(BF16), bd(p0)))
    for lvl in range(1, 6):
        for ch in chains:
            t, qp = tq[ch]
            if lvl < 5:
                both = _dot(jnp.concatenate([qp, t], axis=0).astype(BF16), bd(qp))
                tq[ch] = (t + both[CHUNK:], both[:CHUNK])
            else:
                tq[ch] = (t + _dot(t.astype(BF16), bd(qp)), None)

    uw = {}
    for ch in chains:
        tb = tq[ch][0].astype(BF16)
        uw[ch] = (_dot(tb, bd(pre[ch]["vb"])), _dot(tb, bd(pre[ch]["w_in"])))

    for bi in range(nb):
        for hf in range(n_half):
            s = s_ref[bi * n_half + hf]
            for c in range(tc // CHUNK):
                ch = (c, bi, hf)
                u, w = uw[ch]
                pm = _dot(jnp.concatenate([w, pre[ch]["qd"]], axis=0).astype(BF16), s.astype(BF16))
                v_new = u - pm[:CHUNK]
                o = pm[CHUNK:] + _dot(pre[ch]["qkd"], bd(v_new))
                o_ref[bi, c * CHUNK:(c + 1) * CHUNK, hf * HALF:(hf + 1) * HALF] = o
                s = s * pre[ch]["gle"] + bdm_f32 * _dot_tn(pre[ch]["kd"], v_new.astype(BF16))
            s_ref[bi * n_half + hf] = s


def _gdn(q, k, v, beta_x, g_x, bdm, tc):
    b, t, _ = q.shape
    spec = pl.BlockSpec((b, tc, KEY_DIM), lambda ti: (0, ti, 0))
    return pl.pallas_call(
        _gdn_kernel,
        grid=(t // tc,),
        in_specs=[spec] * 5 + [_const_spec((HALF, HALF))],
        out_specs=spec,
        out_shape=jax.ShapeDtypeStruct((b, t, VAL_DIM), F32),
        scratch_shapes=[pltpu.VMEM((b * (KEY_DIM // HALF), HALF, HALF), F32)],
        compiler_params=pltpu.CompilerParams(
            dimension_semantics=("arbitrary",), vmem_limit_bytes=VMEM_LIMIT_BYTES),
        name="gdn",
    )(q, k, v, beta_x, g_x, bdm)


def _merge_kernel(x_ref, yc_ref, o_ref, szg_ref, sd_ref, ng_ref, wgo_ref, wo_ref, fg_ref,
                  wr_ref, br_ref, ones_ref, eye_ref,
                  xo_ref, eid_ref, wts_ref):
    tm = x_ref.shape[0]
    o = o_ref[...]
    ms = _dot((o * o).astype(BF16), ones_ref[...]) * (1.0 / HEAD_V)
    on = o * lax.rsqrt(ms + EPS) * ng_ref[...] * szg_ref[...].astype(F32)
    y_delta = _dot(on.astype(BF16), wgo_ref[...])
    y = yc_ref[...].astype(F32) + sd_ref[...].astype(F32) * y_delta
    x_new = x_ref[...] + _dot(y.astype(BF16), wo_ref[...])
    _rows_to_tiles(xo_ref, x_new)

    h2 = _rms_rows(x_new, fg_ref[...]).astype(BF16)
    logits = _dot_nt(wr_ref[...], h2) + br_ref[...]
    gl = logits[N_EXPERTS:N_EXPERTS + SUBLANES]
    grow = lax.broadcasted_iota(I32, (SUBLANES, tm), 0)
    gl = jnp.where(grow < N_GROUPS, gl, -jnp.inf)
    gmax = jnp.max(gl, axis=0, keepdims=True)
    group_p = 1.0 / jnp.sum(jnp.exp(gl - gmax), axis=0, keepdims=True)
    gi = jnp.min(jnp.where(gl == gmax, grow, SUBLANES), axis=0, keepdims=True)
    sel = jnp.zeros((EXPERTS_PER_GROUP, tm), F32)
    for g in range(N_GROUPS):
        sel = jnp.where(gi == g, logits[g * EXPERTS_PER_GROUP:(g + 1) * EXPERTS_PER_GROUP], sel)
    erow = lax.broadcasted_iota(I32, (EXPERTS_PER_GROUP, tm), 0)
    m1 = jnp.max(sel, axis=0, keepdims=True)
    i1 = jnp.min(jnp.where(sel == m1, erow, EXPERTS_PER_GROUP), axis=0, keepdims=True)
    sel2 = jnp.where(erow == i1, -jnp.inf, sel)
    m2 = jnp.max(sel2, axis=0, keepdims=True)
    i2 = jnp.min(jnp.where(sel2 == m2, erow, EXPERTS_PER_GROUP), axis=0, keepdims=True)
    e21 = jnp.exp(m2 - m1)
    w1 = group_p / (1.0 + e21)
    w2 = w1 * e21
    eid_ref[...] = jnp.concatenate(
        [gi * EXPERTS_PER_GROUP + i1, gi * EXPERTS_PER_GROUP + i2], axis=0)

    wrow = jnp.concatenate([w1, w2], axis=0)
    p1 = wrow.astype(BF16)
    r1 = wrow - p1.astype(F32)
    p2 = r1.astype(BF16)
    p3 = (r1 - p2.astype(F32)).astype(BF16)
    parts = jnp.concatenate([p1, p2, p3, jnp.zeros((2, tm), BF16)], axis=0)
    cols = _dot_nt(eye_ref[...], parts)
    wcol = cols[:, 0:2] + cols[:, 2:4] + cols[:, 4:6]
    wts_ref[...] = jnp.concatenate([wcol, jnp.zeros((tm, SUBLANES - 2), F32)], axis=1)


def _merge(x2, yc, o, szg, sd, ng_x, wgo, wo, fg, wr_t, br_t, ones_bd, eye, tm):
    n, d = x2.shape
    grid = (n // tm,)
    tile = lambda width: pl.BlockSpec((tm, width), lambda i: (i, 0))
    return pl.pallas_call(
        _merge_kernel,
        grid=grid,
        in_specs=[
            tile(d), tile(d), tile(VAL_DIM), tile(VAL_DIM), tile(d),
            _const_spec((1, VAL_DIM)),
            _const_spec((VAL_DIM, d)),
            _const_spec((d, d)),
            _const_spec((1, d)),
            _const_spec((ROUTER_ROWS, d)),
            _const_spec((ROUTER_ROWS, 1)),
            _const_spec((VAL_DIM, VAL_DIM)),
            _const_spec((tm, tm)),
        ],
        out_specs=(pl.BlockSpec((tm * SUBLANES, LANES), lambda i: (i, 0)),
                   pl.BlockSpec((2, tm), lambda i: (0, i)),
                   pl.BlockSpec((tm, SUBLANES), lambda i: (i, 0))),
        out_shape=(jax.ShapeDtypeStruct((n * SUBLANES, LANES), F32),
                   jax.ShapeDtypeStruct((2, n), I32),
                   jax.ShapeDtypeStruct((n, SUBLANES), F32)),
        compiler_params=pltpu.CompilerParams(
            dimension_semantics=("arbitrary",), vmem_limit_bytes=VMEM_LIMIT_BYTES),
        name="merge",
    )(x2, yc, o, szg, sd, ng_x, wgo, wo, fg, wr_t, br_t, ones_bd, eye)


def _plan_kernel(eid_ref, su_ref, lt_ref, pos_ref, te_ref, pad_ref, carry, offs, *, n_rows):
    tp = eid_ref.shape[1]
    ph = pl.program_id(0)
    i = pl.program_id(1)

    @pl.when(jnp.logical_and(ph == 0, i == 0))
    def _():
        carry[...] = jnp.zeros(carry.shape, F32)

    @pl.when(jnp.logical_and(ph == 1, i == 0))
    def _():
        count = carry[...]
        tiles = jnp.floor((count + (ROW_TILE - 1)) * (1.0 / ROW_TILE))
        hi = jnp.floor(tiles * (1.0 / SPLIT))
        lo = tiles - hi * SPLIT
        ex = _dot(lt_ref[...], jnp.concatenate([hi, lo], axis=1).astype(BF16))
        start = ex[:, :LANES] * SPLIT + ex[:, LANES:]
        offs[...] = start * ROW_TILE
        end = (start + tiles)[:, 0:1]
        tile_id = lax.broadcasted_iota(I32, (N_EXPERTS, te_ref.shape[1]), 1).astype(F32)
        te_ref[...] = jnp.sum(jnp.where(tile_id >= end, 1.0, 0.0), axis=0, keepdims=True).astype(I32)
        on_lane = (lax.broadcasted_iota(I32, (N_EXPERTS, LANES), 0)
                   == lax.broadcasted_iota(I32, (N_EXPERTS, LANES), 1))
        last = lax.broadcasted_iota(I32, (N_EXPERTS, LANES), 0) == N_EXPERTS - 1
        lo_rows = start * ROW_TILE + count
        hi_rows = jnp.where(last, float(n_rows), (start + tiles) * ROW_TILE)
        pad_ref[0:1, :] = jnp.sum(jnp.where(on_lane, lo_rows, 0.0), axis=0, keepdims=True).astype(I32)
        pad_ref[1:2, :] = jnp.sum(jnp.where(on_lane, hi_rows, 0.0), axis=0, keepdims=True).astype(I32)
        carry[...] = jnp.zeros(carry.shape, F32)

    e = eid_ref[...]
    rows = lax.broadcasted_iota(I32, (N_EXPERTS, tp), 0)
    hit0 = rows == e[0:1]
    hit1 = rows == e[1:2]
    hits = jnp.where(hit0 | hit1, 1.0, 0.0)

    @pl.when(ph == 1)
    def _():
        row = _dot(hits.astype(BF16), su_ref[...]) + (carry[:, 0:1] + offs[:, 0:1])
        r0 = jnp.sum(jnp.where(hit0, row, 0.0), axis=0, keepdims=True)
        r1 = jnp.sum(jnp.where(hit1, row, 0.0), axis=0, keepdims=True)
        pos_ref[...] = jnp.concatenate([r0, r1], axis=0).astype(I32)

    carry[...] = carry[...] + jnp.sum(hits, axis=1, keepdims=True)


def _plan(eid, su, lt, tp, te_pad, n_rows):
    n = eid.shape[1]
    return pl.pallas_call(
        functools.partial(_plan_kernel, n_rows=n_rows),
        grid=(2, n // tp),
        in_specs=[pl.BlockSpec((2, tp), lambda ph, i: (0, i)),
                  _const_spec((tp, tp)), _const_spec((N_EXPERTS, N_EXPERTS))],
        out_specs=(pl.BlockSpec((2, tp), lambda ph, i: (0, i * ph)),
                   pl.BlockSpec((1, te_pad), lambda ph, i: (0, 0)),
                   pl.BlockSpec((2, LANES), lambda ph, i: (0, 0))),
        out_shape=(jax.ShapeDtypeStruct((2, n), I32),
                   jax.ShapeDtypeStruct((1, te_pad), I32),
                   jax.ShapeDtypeStruct((2, LANES), I32)),
        scratch_shapes=[pltpu.VMEM((N_EXPERTS, LANES), F32), pltpu.VMEM((N_EXPERTS, LANES), F32)],
        compiler_params=pltpu.CompilerParams(dimension_semantics=("arbitrary", "arbitrary")),
        name="plan",
    )(eid, su, lt)


def _tile_dma(src_hbm, row, dst, r, sem):
    dst_rows = r * SUBLANES if isinstance(r, int) else pl.multiple_of(r * SUBLANES, SUBLANES)
    return pltpu.make_async_copy(
        src_hbm.at[pl.ds(pl.multiple_of(row * SUBLANES, SUBLANES), SUBLANES)],
        dst.at[pl.ds(dst_rows, SUBLANES)], sem)


def _gather_tiles(n, row_of, src_hbm, dsts, sem, inline):
    def start(r, j):
        for k, dst in enumerate(dsts):
            _tile_dma(src_hbm, row_of(k, r), dst, r, sem).start(priority=(j + k) % 2)

    if inline:
        for r in range(n):
            start(r, r)
    else:
        def block(blk, carry):
            for j in range(ISSUE_UNROLL):
                start(blk * ISSUE_UNROLL + j, j)
            return carry
        lax.fori_loop(0, n // ISSUE_UNROLL, block, 0)


def _expert_kernel(te_ref, pos_ref, pad_ref, x_hbm, fg_ref, wg_ref, wu_ref, wd_ref, ys_ref,
                   *scratch, n_tiles, n_tokens):
    bufs, sems, inv = scratch[:N_XBUF], scratch[N_XBUF], scratch[N_XBUF + 1]
    i = pl.program_id(0)

    @pl.when(i == 0)
    def _():
        def clear_expert(e, carry):
            def clear(p, c):
                inv[p] = 0
                return c
            return lax.fori_loop(pad_ref[e], pad_ref[LANES + e], clear, carry)
        lax.fori_loop(0, N_EXPERTS, clear_expert, 0)

        def scatter(t, carry):
            inv[pos_ref[t]] = t
            inv[pos_ref[n_tokens + t]] = t
            return carry
        lax.fori_loop(0, n_tokens, scatter, 0, unroll=8)

    live = te_ref[i] < N_EXPERTS
    pending = te_ref[jnp.maximum(i - GATHER_AHEAD, 0)] < N_EXPERTS
    can_start = i + GATHER_AHEAD < n_tiles

    def gather(tile, p, inline):
        _gather_tiles(ROW_TILE, lambda k, r: inv[tile * ROW_TILE + r], x_hbm, [bufs[p]],
                      sems.at[p], inline)

    def gather_wait(p):
        pltpu.make_async_copy(x_hbm.at[pl.ds(0, ROW_TILE * SUBLANES)], bufs[p], sems.at[p]).wait()

    def compute(p):
        x = _rows_from_tiles(bufs[p], ROW_TILE)
        h = _rms_rows(x, fg_ref[...]).astype(BF16)
        hid = _silu(_dot(h, wg_ref[0].astype(BF16))) * _dot(h, wu_ref[0].astype(BF16))
        _rows_to_tiles(ys_ref, _dot(hid.astype(BF16), wd_ref[0].astype(BF16)))

    @pl.when(jnp.logical_and(i == 0, live))
    def _():
        for tile in range(min(GATHER_AHEAD, n_tiles)):
            gather(tile, tile % N_XBUF, False)

    def residue_step(p):
        mine = (i % N_XBUF) == p

        @pl.when(mine & live & can_start)
        def _():
            gather_wait(p)
            gather(i + GATHER_AHEAD, (p + GATHER_AHEAD) % N_XBUF, True)
            compute(p)

        if any((n_tiles - back) % N_XBUF == p for back in range(1, GATHER_AHEAD + 1)):
            @pl.when(mine & live & jnp.logical_not(can_start))
            def _():
                gather_wait(p)
                compute(p)

        @pl.when(mine & jnp.logical_not(live) & pending)
        def _():
            gather_wait(p)

    for p in range(N_XBUF):
        residue_step(p)

    @pl.when(jnp.logical_not(live))
    def _():
        ys_ref[...] = jnp.zeros(ys_ref.shape, F32)


def _experts(tile_expert, pos_flat, pad_flat, x_tiles, fg, wg, wu, wd, n_rows, layer):
    d = fg.shape[1]
    expert = lambda i, te, pos, pad: (layer, jnp.minimum(te[i], N_EXPERTS - 1), 0, 0)
    grid_spec = pltpu.PrefetchScalarGridSpec(
        num_scalar_prefetch=3,
        grid=(n_rows // ROW_TILE,),
        in_specs=[
            pl.BlockSpec(memory_space=pl.ANY),
            pl.BlockSpec((1, d), lambda i, te, pos, pad: (0, 0)),
            pl.BlockSpec((None, 1, d, D_EXPERT), expert),
            pl.BlockSpec((None, 1, d, D_EXPERT), expert),
            pl.BlockSpec((None, 1, D_EXPERT, d), expert),
        ],
        out_specs=pl.BlockSpec((ROW_TILE * SUBLANES, LANES), lambda i, te, pos, pad: (i, 0)),
        scratch_shapes=[pltpu.VMEM((ROW_TILE * SUBLANES, LANES), F32)] * N_XBUF
        + [pltpu.SemaphoreType.DMA((N_XBUF,)), pltpu.SMEM((n_rows,), I32)],
    )
    return pl.pallas_call(
        functools.partial(_expert_kernel, n_tiles=n_rows // ROW_TILE,
                          n_tokens=pos_flat.shape[0] // 2),
        grid_spec=grid_spec,
        out_shape=jax.ShapeDtypeStruct((n_rows * SUBLANES, LANES), F32),
        compiler_params=pltpu.CompilerParams(
            dimension_semantics=("arbitrary",), vmem_limit_bytes=VMEM_LIMIT_BYTES),
        name="experts",
    )(tile_expert, pos_flat, pad_flat, x_tiles, fg, wg, wu, wd)


def _combine_kernel(pos_ref, x_ref, w_ref, fin_ref, ys_hbm, out_ref, ybuf0, ybuf1, sems,
                    *, n_tokens, tc, n_steps, final_norm):
    i = pl.program_id(0)
    bufs = (ybuf0, ybuf1)

    def gather(step, p, inline):
        _gather_tiles(tc, lambda slot, r: pos_ref[slot * n_tokens + step * tc + r], ys_hbm,
                      [bufs[p].at[0], bufs[p].at[1]], sems.at[p], inline)

    def gather_wait(p):
        for slot in range(2):
            pltpu.make_async_copy(ys_hbm.at[pl.ds(0, tc * SUBLANES)], bufs[p].at[slot],
                                  sems.at[p]).wait()

    def compute(p):
        w = w_ref[...]
        out = (_rows_from_tiles(x_ref, tc) + w[:, 0:1] * _rows_from_tiles(bufs[p].at[0], tc)
               + w[:, 1:2] * _rows_from_tiles(bufs[p].at[1], tc))
        if final_norm:
            out = _rms_rows(out, fin_ref[...])
        out_ref[...] = out

    @pl.when(i == 0)
    def _():
        gather(0, 0, False)

    def parity_step(p):
        mine = (i % 2) == p

        @pl.when(mine & (i + 1 < n_steps))
        def _():
            gather_wait(p)
            gather(i + 1, 1 - p, True)
            compute(p)

        if (n_steps - 1) % 2 == p:
            @pl.when(mine & (i + 1 == n_steps))
            def _():
                gather_wait(p)
                compute(p)

    parity_step(0)
    parity_step(1)


def _combine(pos_flat, x_tiles, wts, fin_gain, ys_tiles, tc, final_norm):
    n = wts.shape[0]
    d = fin_gain.shape[1]
    grid_spec = pltpu.PrefetchScalarGridSpec(
        num_scalar_prefetch=1,
        grid=(n // tc,),
        in_specs=[
            pl.BlockSpec((tc * SUBLANES, LANES), lambda i, pos: (i, 0)),
            pl.BlockSpec((tc, SUBLANES), lambda i, pos: (i, 0)),
            pl.BlockSpec((1, d), lambda i, pos: (0, 0)),
            pl.BlockSpec(memory_space=pl.ANY),
        ],
        out_specs=pl.BlockSpec((tc, d), lambda i, pos: (i, 0)),
        scratch_shapes=[pltpu.VMEM((2, tc * SUBLANES, LANES), F32),
                        pltpu.VMEM((2, tc * SUBLANES, LANES), F32),
                        pltpu.SemaphoreType.DMA((2,))],
    )
    return pl.pallas_call(
        functools.partial(_combine_kernel, n_tokens=n, tc=tc, n_steps=n // tc,
                          final_norm=final_norm),
        grid_spec=grid_spec,
        out_shape=jax.ShapeDtypeStruct((n, d), F32),
        compiler_params=pltpu.CompilerParams(
            dimension_semantics=("arbitrary",), vmem_limit_bytes=VMEM_LIMIT_BYTES),
        name="combine",
    )(pos_flat, x_tiles, wts, fin_gain, ys_tiles)


def _tile(n, target):
    t = min(n, target)
    while n % t:
        t //= 2
    return t


def _widen_w_in(w):
    o = 0
    conv = w[:, o:o + 2 * CONV_WIDTH]; o += 2 * CONV_WIDTH
    qkv = w[:, o:o + QKV_DIM]; o += QKV_DIM
    zg = w[:, o:o + VAL_DIM]; o += VAL_DIM
    wb = w[:, o:o + N_HEADS]; o += N_HEADS
    wa = w[:, o:o + N_HEADS]; o += N_HEADS
    gc = w[:, o:o + D_MODEL]; o += D_MODEL
    gd = w[:, o:o + D_MODEL]
    return jnp.concatenate(
        [conv, qkv, zg, jnp.repeat(wb, HEAD_K, axis=1), jnp.repeat(wa, HEAD_K, axis=1), gc, gd],
        axis=1).astype(BF16)


def kernel(x, norm_mix_gain, w_in, conv_dw_w, conv_dw_b, conv_ln_gain, conv_ln_bias, w_conv_out,
           gdn_conv_w, gdn_a_log, gdn_dt_bias, gdn_norm_gain, w_gdn_out, w_out,
           norm_ffn_gain, w_router_group, b_router_group, w_router_expert, b_router_expert,
           w_expert_gate, w_expert_up, w_expert_down, final_norm_gain):
    b, t, d = x.shape
    n = b * t
    depth = w_in.shape[0]
    tm = _tile(t, 512)
    tc_gdn = _tile(t, CHUNK)
    tm_merge = _tile(n, 512)
    tp = _tile(n, 512)
    tc_comb = _tile(n, 256)
    n_rows = -(-(2 * n + N_EXPERTS * (ROW_TILE - 1)) // ROW_TILE) * ROW_TILE
    te_pad = -(-(n_rows // ROW_TILE) // LANES) * LANES

    head_of_lane = jnp.arange(KEY_DIM) // HEAD_K
    ones_bd = (head_of_lane[:, None] == head_of_lane[None, :]).astype(BF16)
    bdm = ones_bd[:HALF, :HALF]
    eye = jnp.eye(tm_merge, dtype=BF16)
    su = (jnp.arange(tp)[:, None] < jnp.arange(tp)[None, :]).astype(BF16)
    lt = (jnp.arange(N_EXPERTS)[:, None] > jnp.arange(N_EXPERTS)[None, :]).astype(BF16)
    row = lambda a: a.reshape(1, -1).astype(F32)

    for l in range(depth):
        yc, q, k, v, beta_x, g_x, szg, sd = _inproj(
            x, row(norm_mix_gain[l]), _widen_w_in(w_in[l]), conv_dw_w[l], row(conv_dw_b[l]),
            row(conv_ln_gain[l]), row(conv_ln_bias[l]), w_conv_out[l].astype(BF16), gdn_conv_w[l],
            row(jnp.repeat(gdn_a_log[l], HEAD_K)), row(jnp.repeat(gdn_dt_bias[l], HEAD_K)),
            ones_bd, tm)
        o = _gdn(q, k, v, beta_x, g_x, bdm, tc_gdn)

        wr_t = jnp.concatenate(
            [w_router_expert[l].T, w_router_group[l].T,
             jnp.zeros((ROUTER_ROWS - N_EXPERTS - N_GROUPS, d), F32)], axis=0).astype(BF16)
        br_t = jnp.concatenate(
            [b_router_expert[l], b_router_group[l],
             jnp.zeros((ROUTER_ROWS - N_EXPERTS - N_GROUPS,), F32)]).reshape(ROUTER_ROWS, 1)
        x_mid, eid, wts = _merge(
            x.reshape(n, d), yc.reshape(n, d), o.reshape(n, VAL_DIM), szg.reshape(n, VAL_DIM),
            sd.reshape(n, d), row(jnp.tile(gdn_norm_gain[l], N_HEADS)), w_gdn_out[l].astype(BF16),
            w_out[l].astype(BF16), row(norm_ffn_gain[l]), wr_t, br_t, ones_bd, eye, tm_merge)

        pos, tile_expert, pad_rows = _plan(eid, su, lt, tp, te_pad, n_rows)
        pos_flat = pos.reshape(2 * n)
        ys = _experts(tile_expert.reshape(te_pad), pos_flat, pad_rows.reshape(2 * LANES), x_mid,
                      row(norm_ffn_gain[l]), w_expert_gate, w_expert_up, w_expert_down, n_rows, l)
        last = l == depth - 1
        x = _combine(pos_flat, x_mid, wts, row(final_norm_gain), ys, tc_comb, last
                     ).reshape(b, t, d)
    return x
```

```python
import functools

import jax
import jax.numpy as jnp
from jax import lax
from jax.experimental import pallas as pl
from jax.experimental.pallas import tpu as pltpu

F32 = jnp.float32
BF16 = jnp.bfloat16
I32 = jnp.int32

D_MODEL = 1024
CONV_WIDTH = 512
CONV_KERNEL = 31
N_HEADS = 8
HEAD_K = 64
HEAD_V = 64
KEY_DIM = N_HEADS * HEAD_K
VAL_DIM = N_HEADS * HEAD_V
QKV_DIM = 2 * KEY_DIM + VAL_DIM
SHORT_CONV = 4
CHUNK = 64
N_GROUPS = 4
EXPERTS_PER_GROUP = 8
N_EXPERTS = N_GROUPS * EXPERTS_PER_GROUP
D_EXPERT = 256
EPS = 1e-6

SUBLANES = 8
LANES = 128
VMEM_LIMIT_BYTES = 56 * 1024 * 1024

CONV_HALO = 32
QKV_HALO = SUBLANES
HALF = 4 * HEAD_K
ROW_TILE = 256
ROUTER_ROWS = 40
ISSUE_UNROLL = 8
GATHER_AHEAD = 2
N_XBUF = GATHER_AHEAD + 1
SPLIT = 16

C_CONV = 0
C_QKV = C_CONV + 2 * CONV_WIDTH
C_ZG = C_QKV + QKV_DIM
C_BETA = C_ZG + VAL_DIM
C_DECAY = C_BETA + KEY_DIM
C_GATE_C = C_DECAY + KEY_DIM
C_GATE_D = C_GATE_C + D_MODEL
D_IN_WIDE = C_GATE_D + D_MODEL


def _dot(a, b):
    return jnp.dot(a, b, preferred_element_type=F32)


def _dot_nt(a, b):
    return lax.dot_general(a, b, (((1,), (1,)), ((), ())), preferred_element_type=F32)


def _dot_tn(a, b):
    return lax.dot_general(a, b, (((0,), (0,)), ((), ())), preferred_element_type=F32)


def _sigmoid(x):
    return 0.5 * jnp.tanh(0.5 * x) + 0.5


def _silu(x):
    return x * _sigmoid(x)


def _rms_rows(x, gain):
    return x * lax.rsqrt(jnp.mean(x * x, axis=-1, keepdims=True) + EPS) * gain


def _const_spec(shape):
    zeros = (0,) * len(shape)
    return pl.BlockSpec(shape, lambda *_: zeros, pipeline_mode=pl.Buffered(1))


def _rows_to_tiles(ref, val):
    rows = val.shape[0]
    for cb in range(val.shape[1] // LANES):
        ref[pl.ds(cb, rows, stride=SUBLANES), :] = val[:, cb * LANES:(cb + 1) * LANES]


def _rows_from_tiles(ref, rows):
    return jnp.concatenate(
        [ref[pl.ds(cb, rows, stride=SUBLANES), :] for cb in range(SUBLANES)], axis=1)


def _shift_up(a, r):
    return a if r == 0 else pltpu.roll(a, a.shape[0] - r, 0)


def _inproj_kernel(x_ref, gain_ref, w_ref, dww_ref, dwb_ref, lng_ref, lnb_ref, wco_ref,
                   scw_ref, alog_ref, dtb_ref, ones_ref,
                   yc_ref, q_ref, k_ref, v_ref, beta_ref, g_ref, szg_ref, sd_ref,
                   ubuf, qbuf):
    tm = x_ref.shape[1]
    ext = tm + SUBLANES

    @pl.when(pl.program_id(1) == 0)
    def _():
        ubuf[0:CONV_HALO, :] = jnp.zeros((CONV_HALO, CONV_WIDTH), F32)
        ubuf[CONV_HALO + tm:, :] = jnp.zeros((SUBLANES, CONV_WIDTH), F32)
        qbuf[0:QKV_HALO, :] = jnp.zeros((QKV_HALO, QKV_DIM), F32)
        qbuf[QKV_HALO + tm:, :] = jnp.zeros((SUBLANES, QKV_DIM), F32)

    h = _rms_rows(x_ref[0], gain_ref[...]).astype(BF16)

    c = _dot(h, w_ref[:, C_CONV:C_QKV])
    ubuf[CONV_HALO:CONV_HALO + tm, :] = c[:, :CONV_WIDTH] * _sigmoid(c[:, CONV_WIDTH:])
    first = CONV_HALO - (CONV_KERNEL - 1)
    acc = jnp.zeros((tm, CONV_WIDTH), F32) + dwb_ref[...]
    for r in range(SUBLANES):
        part = None
        for j in range(CONV_KERNEL):
            s = first + j
            if s % SUBLANES != r:
                continue
            term = ubuf[s - r:s - r + ext, :] * dww_ref[j:j + 1, :]
            part = term if part is None else part + term
        if part is not None:
            acc = acc + _shift_up(part, r)[0:tm]
    ubuf[0:CONV_HALO, :] = ubuf[tm:tm + CONV_HALO, :]
    mu = jnp.mean(acc, axis=-1, keepdims=True)
    xc = acc - mu
    ln = xc * lax.rsqrt(jnp.mean(xc * xc, axis=-1, keepdims=True) + EPS) * lng_ref[...] + lnb_ref[...]
    y_conv = _dot(_silu(ln).astype(BF16), wco_ref[...])
    gate_c = _sigmoid(_dot(h, w_ref[:, C_GATE_C:C_GATE_D]))
    yc_ref[0] = (gate_c * y_conv).astype(yc_ref.dtype)

    qbuf[QKV_HALO:QKV_HALO + tm, :] = _dot(h, w_ref[:, C_QKV:C_ZG])
    first = QKV_HALO - (SHORT_CONV - 1)
    qkv = jnp.zeros((tm, QKV_DIM), F32)
    for j in range(SHORT_CONV):
        s = first + j
        r = s % SUBLANES
        term = qbuf[s - r:s - r + ext, :] * scw_ref[j:j + 1, :]
        qkv = qkv + _shift_up(term, r)[0:tm]
    qbuf[0:QKV_HALO, :] = qbuf[tm:tm + QKV_HALO, :]
    qkv = _silu(qkv)
    q = qkv[:, :KEY_DIM]
    k = qkv[:, KEY_DIM:2 * KEY_DIM]
    ones = ones_ref[...]
    q_ref[0] = q * lax.rsqrt(_dot((q * q).astype(BF16), ones) + EPS) * (HEAD_K ** -0.5)
    k_ref[0] = k * lax.rsqrt(_dot((k * k).astype(BF16), ones) + EPS)
    v_ref[0] = qkv[:, 2 * KEY_DIM:]

    szg_ref[0] = _silu(_dot(h, w_ref[:, C_ZG:C_BETA])).astype(szg_ref.dtype)
    beta_ref[0] = _sigmoid(_dot(h, w_ref[:, C_BETA:C_DECAY]))
    a = _dot(h, w_ref[:, C_DECAY:C_GATE_C]) + dtb_ref[...]
    softplus = jnp.maximum(a, 0.0) + jnp.log1p(jnp.exp(-jnp.abs(a)))
    g_ref[0] = -jnp.exp(alog_ref[...]) * softplus
    sd_ref[0] = _sigmoid(_dot(h, w_ref[:, C_GATE_D:D_IN_WIDE])).astype(sd_ref.dtype)


def _inproj(x, gain, w_wide, dww, dwb, lng, lnb, wco, scw, alog_x, dtb_x, ones_bd, tm):
    b, t, d = x.shape
    grid = (b, t // tm)
    tile = lambda width: pl.BlockSpec((1, tm, width), lambda bi, ti: (bi, ti, 0))
    out_shape = (
        jax.ShapeDtypeStruct((b, t, D_MODEL), BF16),
        jax.ShapeDtypeStruct((b, t, KEY_DIM), F32),
        jax.ShapeDtypeStruct((b, t, KEY_DIM), F32),
        jax.ShapeDtypeStruct((b, t, VAL_DIM), F32),
        jax.ShapeDtypeStruct((b, t, KEY_DIM), F32),
        jax.ShapeDtypeStruct((b, t, KEY_DIM), F32),
        jax.ShapeDtypeStruct((b, t, VAL_DIM), BF16),
        jax.ShapeDtypeStruct((b, t, D_MODEL), BF16),
    )
    return pl.pallas_call(
        _inproj_kernel,
        grid=grid,
        in_specs=[
            tile(d),
            _const_spec((1, d)),
            _const_spec((d, D_IN_WIDE)),
            _const_spec((CONV_KERNEL, CONV_WIDTH)),
            _const_spec((1, CONV_WIDTH)),
            _const_spec((1, CONV_WIDTH)),
            _const_spec((1, CONV_WIDTH)),
            _const_spec((CONV_WIDTH, D_MODEL)),
            _const_spec((SHORT_CONV, QKV_DIM)),
            _const_spec((1, KEY_DIM)),
            _const_spec((1, KEY_DIM)),
            _const_spec((KEY_DIM, KEY_DIM)),
        ],
        out_specs=(tile(D_MODEL), tile(KEY_DIM), tile(KEY_DIM), tile(VAL_DIM), tile(KEY_DIM),
                   tile(KEY_DIM), tile(VAL_DIM), tile(D_MODEL)),
        out_shape=out_shape,
        scratch_shapes=[
            pltpu.VMEM((CONV_HALO + tm + SUBLANES, CONV_WIDTH), F32),
            pltpu.VMEM((QKV_HALO + tm + SUBLANES, QKV_DIM), F32),
        ],
        compiler_params=pltpu.CompilerParams(
            dimension_semantics=("arbitrary", "arbitrary"), vmem_limit_bytes=VMEM_LIMIT_BYTES),
        name="inproj",
    )(x, gain, w_wide, dww, dwb, lng, lnb, wco, scw, alog_x, dtb_x, ones_bd)


def _gdn_kernel(q_ref, k_ref, v_ref, b_ref, g_ref, bdm_ref, o_ref, s_ref):
    nb, tc = q_ref.shape[0], q_ref.shape[1]
    n_half = KEY_DIM // HALF

    @pl.when(pl.program_id(0) == 0)
    def _():
        s_ref[...] = jnp.zeros(s_ref.shape, F32)

    row = lax.broadcasted_iota(I32, (CHUNK, HALF), 0)
    col = lax.broadcasted_iota(I32, (CHUNK, HALF), 1) & (CHUNK - 1)
    diag = row == col
    causal = row >= col
    strict = row > col
    tri = (lax.broadcasted_iota(I32, (CHUNK, CHUNK), 0)
           >= lax.broadcasted_iota(I32, (CHUNK, CHUNK), 1)).astype(BF16)
    bdm = bdm_ref[...]
    bdm_f32 = bdm.astype(F32)

    def bd(m):
        mb = m.astype(BF16)
        return jnp.concatenate([mb, mb, mb, mb], axis=0) * bdm

    chains = [(c, bi, hf) for c in range(tc // CHUNK) for bi in range(nb) for hf in range(n_half)]

    pre = {}
    for ch in chains:
        c, bi, hf = ch
        rows = slice(c * CHUNK, (c + 1) * CHUNK)
        lanes = slice(hf * HALF, (hf + 1) * HALF)
        q = q_ref[bi, rows, lanes]
        k = k_ref[bi, rows, lanes]
        v = v_ref[bi, rows, lanes]
        bx = b_ref[bi, rows, lanes]
        gx = g_ref[bi, rows, lanes]
        g1 = gx.astype(BF16)
        r1 = gx - g1.astype(F32)
        g2 = r1.astype(BF16)
        g3 = (r1 - g2.astype(F32)).astype(BF16)
        gc3 = _dot(tri, jnp.concatenate([g1, g2, g3], axis=1))
        gc = gc3[:, :HALF] + gc3[:, HALF:2 * HALF] + gc3[:, 2 * HALF:]
        gl = gc[CHUNK - 1:CHUNK, :]
        eg = jnp.exp(gc)
        kb = k * bx
        gc_t = jnp.sum(jnp.where(diag, gc, 0.0), axis=0, keepdims=True)
        decay = jnp.where(causal, jnp.exp(jnp.minimum(gc - gc_t, 0.0)), 0.0)
        s1 = _dot_nt(jnp.concatenate([k, q], axis=0).astype(BF16), bd(k))
        p0 = jnp.where(strict, -(s1[:CHUNK] * bx * decay), 0.0)
        pre[ch] = dict(
            p0=p0, qkd=(s1[CHUNK:] * decay).astype(BF16), vb=v * bx, w_in=kb * eg,
            qd=q * eg, kd=(k * jnp.exp(gl - gc)).astype(BF16), gle=jnp.exp(gl))

    tq = {}
    for ch in chains:
        p0 = pre[ch]["p0"]
        tq[ch] = (jnp.where(diag, 1.0, 0.0) + p0, _dot(p0.astype(BF16), bd(p0)))
    for lvl in range(1, 6):
        for ch in chains:
            t, qp = tq[ch]
            if lvl < 5:
                both = _dot(jnp.concatenate([qp, t], axis=0).astype(BF16), bd(qp))
                tq[ch] = (t + both[CHUNK:], both[:CHUNK])
            else:
                tq[ch] = (t + _dot(t.astype(BF16), bd(qp)), None)

    uw = {}
    for ch in chains:
        tb = tq[ch][0].astype(BF16)
        uw[ch] = (_dot(tb, bd(pre[ch]["vb"])), _dot(tb, bd(pre[ch]["w_in"])))

    for bi in range(nb):
        for hf in range(n_half):
            s = s_ref[bi * n_half + hf]
            for c in range(tc // CHUNK):
                ch = (c, bi, hf)
                u, w = uw[ch]
                pm = _dot(jnp.concatenate([w, pre[ch]["qd"]], axis=0).astype(BF16), s.astype(BF16))
                v_new = u - pm[:CHUNK]
                o = pm[CHUNK:] + _dot(pre[ch]["qkd"], bd(v_new))
                o_ref[bi, c * CHUNK:(c + 1) * CHUNK, hf * HALF:(hf + 1) * HALF] = o
                s = s * pre[ch]["gle"] + bdm_f32 * _dot_tn(pre[ch]["kd"], v_new.astype(BF16))
            s_ref[bi * n_half + hf] = s


def _gdn(q, k, v, beta_x, g_x, bdm, tc):
    b, t, _ = q.shape
    spec = pl.BlockSpec((b, tc, KEY_DIM), lambda ti: (0, ti, 0))
    return pl.pallas_call(
        _gdn_kernel,
        grid=(t // tc,),
        in_specs=[spec] * 5 + [_const_spec((HALF, HALF))],
        out_specs=spec,
        out_shape=jax.ShapeDtypeStruct((b, t, VAL_DIM), F32),
        scratch_shapes=[pltpu.VMEM((b * (KEY_DIM // HALF), HALF, HALF), F32)],
        compiler_params=pltpu.CompilerParams(
            dimension_semantics=("arbitrary",), vmem_limit_bytes=VMEM_LIMIT_BYTES),
        name="gdn",
    )(q, k, v, beta_x, g_x, bdm)


def _merge_kernel(x_ref, yc_ref, o_ref, szg_ref, sd_ref, ng_ref, wgo_ref, wo_ref, fg_ref,
                  wr_ref, br_ref, ones_ref, eye_ref,
                  xo_ref, eid_ref, wts_ref):
    tm = x_ref.shape[0]
    o = o_ref[...]
    ms = _dot((o * o).astype(BF16), ones_ref[...]) * (1.0 / HEAD_V)
    on = o * lax.rsqrt(ms + EPS) * ng_ref[...] * szg_ref[...].astype(F32)
    y_delta = _dot(on.astype(BF16), wgo_ref[...])
    y = yc_ref[...].astype(F32) + sd_ref[...].astype(F32) * y_delta
    x_new = x_ref[...] + _dot(y.astype(BF16), wo_ref[...])
    _rows_to_tiles(xo_ref, x_new)

    h2 = _rms_rows(x_new, fg_ref[...]).astype(BF16)
    logits = _dot_nt(wr_ref[...], h2) + br_ref[...]
    gl = logits[N_EXPERTS:N_EXPERTS + SUBLANES]
    grow = lax.broadcasted_iota(I32, (SUBLANES, tm), 0)
    gl = jnp.where(grow < N_GROUPS, gl, -jnp.inf)
    gmax = jnp.max(gl, axis=0, keepdims=True)
    group_p = 1.0 / jnp.sum(jnp.exp(gl - gmax), axis=0, keepdims=True)
    gi = jnp.min(jnp.where(gl == gmax, grow, SUBLANES), axis=0, keepdims=True)
    sel = jnp.zeros((EXPERTS_PER_GROUP, tm), F32)
    for g in range(N_GROUPS):
        sel = jnp.where(gi == g, logits[g * EXPERTS_PER_GROUP:(g + 1) * EXPERTS_PER_GROUP], sel)
    erow = lax.broadcasted_iota(I32, (EXPERTS_PER_GROUP, tm), 0)
    m1 = jnp.max(sel, axis=0, keepdims=True)
    i1 = jnp.min(jnp.where(sel == m1, erow, EXPERTS_PER_GROUP), axis=0, keepdims=True)
    sel2 = jnp.where(erow == i1, -jnp.inf, sel)
    m2 = jnp.max(sel2, axis=0, keepdims=True)
    i2 = jnp.min(jnp.where(sel2 == m2, erow, EXPERTS_PER_GROUP), axis=0, keepdims=True)
    e21 = jnp.exp(m2 - m1)
    w1 = group_p / (1.0 + e21)
    w2 = w1 * e21
    eid_ref[...] = jnp.concatenate(
        [gi * EXPERTS_PER_GROUP + i1, gi * EXPERTS_PER_GROUP + i2], axis=0)

    wrow = jnp.concatenate([w1, w2], axis=0)
    p1 = wrow.astype(BF16)
    r1 = wrow - p1.astype(F32)
    p2 = r1.astype(BF16)
    p3 = (r1 - p2.astype(F32)).astype(BF16)
    parts = jnp.concatenate([p1, p2, p3, jnp.zeros((2, tm), BF16)], axis=0)
    cols = _dot_nt(eye_ref[...], parts)
    wcol = cols[:, 0:2] + cols[:, 2:4] + cols[:, 4:6]
    wts_ref[...] = jnp.concatenate([wcol, jnp.zeros((tm, SUBLANES - 2), F32)], axis=1)


def _merge(x2, yc, o, szg, sd, ng_x, wgo, wo, fg, wr_t, br_t, ones_bd, eye, tm):
    n, d = x2.shape
    grid = (n // tm,)
    tile = lambda width: pl.BlockSpec((tm, width), lambda i: (i, 0))
    return pl.pallas_call(
        _merge_kernel,
        grid=grid,
        in_specs=[
            tile(d), tile(d), tile(VAL_DIM), tile(VAL_DIM), tile(d),
            _const_spec((1, VAL_DIM)),
            _const_spec((VAL_DIM, d)),
            _const_spec((d, d)),
            _const_spec((1, d)),
            _const_spec((ROUTER_ROWS, d)),
            _const_spec((ROUTER_ROWS, 1)),
            _const_spec((VAL_DIM, VAL_DIM)),
            _const_spec((tm, tm)),
        ],
        out_specs=(pl.BlockSpec((tm * SUBLANES, LANES), lambda i: (i, 0)),
                   pl.BlockSpec((2, tm), lambda i: (0, i)),
                   pl.BlockSpec((tm, SUBLANES), lambda i: (i, 0))),
        out_shape=(jax.ShapeDtypeStruct((n * SUBLANES, LANES), F32),
                   jax.ShapeDtypeStruct((2, n), I32),
                   jax.ShapeDtypeStruct((n, SUBLANES), F32)),
        compiler_params=pltpu.CompilerParams(
            dimension_semantics=("arbitrary",), vmem_limit_bytes=VMEM_LIMIT_BYTES),
        name="merge",
    )(x2, yc, o, szg, sd, ng_x, wgo, wo, fg, wr_t, br_t, ones_bd, eye)


def _plan_kernel(eid_ref, su_ref, lt_ref, pos_ref, te_ref, pad_ref, carry, offs, *, n_rows):
    tp = eid_ref.shape[1]
    ph = pl.program_id(0)
    i = pl.program_id(1)

    @pl.when(jnp.logical_and(ph == 0, i == 0))
    def _():
        carry[...] = jnp.zeros(carry.shape, F32)

    @pl.when(jnp.logical_and(ph == 1, i == 0))
    def _():
        count = carry[...]
        tiles = jnp.floor((count + (ROW_TILE - 1)) * (1.0 / ROW_TILE))
        hi = jnp.floor(tiles * (1.0 / SPLIT))
        lo = tiles - hi * SPLIT
        ex = _dot(lt_ref[...], jnp.concatenate([hi, lo], axis=1).astype(BF16))
        start = ex[:, :LANES] * SPLIT + ex[:, LANES:]
        offs[...] = start * ROW_TILE
        end = (start + tiles)[:, 0:1]
        tile_id = lax.broadcasted_iota(I32, (N_EXPERTS, te_ref.shape[1]), 1).astype(F32)
        te_ref[...] = jnp.sum(jnp.where(tile_id >= end, 1.0, 0.0), axis=0, keepdims=True).astype(I32)
        on_lane = (lax.broadcasted_iota(I32, (N_EXPERTS, LANES), 0)
                   == lax.broadcasted_iota(I32, (N_EXPERTS, LANES), 1))
        last = lax.broadcasted_iota(I32, (N_EXPERTS, LANES), 0) == N_EXPERTS - 1
        lo_rows = start * ROW_TILE + count
        hi_rows = jnp.where(last, float(n_rows), (start + tiles) * ROW_TILE)
        pad_ref[0:1, :] = jnp.sum(jnp.where(on_lane, lo_rows, 0.0), axis=0, keepdims=True).astype(I32)
        pad_ref[1:2, :] = jnp.sum(jnp.where(on_lane, hi_rows, 0.0), axis=0, keepdims=True).astype(I32)
        carry[...] = jnp.zeros(carry.shape, F32)

    e = eid_ref[...]
    rows = lax.broadcasted_iota(I32, (N_EXPERTS, tp), 0)
    hit0 = rows == e[0:1]
    hit1 = rows == e[1:2]
    hits = jnp.where(hit0 | hit1, 1.0, 0.0)

    @pl.when(ph == 1)
    def _():
        row = _dot(hits.astype(BF16), su_ref[...]) + (carry[:, 0:1] + offs[:, 0:1])
        r0 = jnp.sum(jnp.where(hit0, row, 0.0), axis=0, keepdims=True)
        r1 = jnp.sum(jnp.where(hit1, row, 0.0), axis=0, keepdims=True)
        pos_ref[...] = jnp.concatenate([r0, r1], axis=0).astype(I32)

    carry[...] = carry[...] + jnp.sum(hits, axis=1, keepdims=True)


def _plan(eid, su, lt, tp, te_pad, n_rows):
    n = eid.shape[1]
    return pl.pallas_call(
        functools.partial(_plan_kernel, n_rows=n_rows),
        grid=(2, n // tp),
        in_specs=[pl.BlockSpec((2, tp), lambda ph, i: (0, i)),
                  _const_spec((tp, tp)), _const_spec((N_EXPERTS, N_EXPERTS))],
        out_specs=(pl.BlockSpec((2, tp), lambda ph, i: (0, i * ph)),
                   pl.BlockSpec((1, te_pad), lambda ph, i: (0, 0)),
                   pl.BlockSpec((2, LANES), lambda ph, i: (0, 0))),
        out_shape=(jax.ShapeDtypeStruct((2, n), I32),
                   jax.ShapeDtypeStruct((1, te_pad), I32),
                   jax.ShapeDtypeStruct((2, LANES), I32)),
        scratch_shapes=[pltpu.VMEM((N_EXPERTS, LANES), F32), pltpu.VMEM((N_EXPERTS, LANES), F32)],
        compiler_params=pltpu.CompilerParams(dimension_semantics=("arbitrary", "arbitrary")),
        name="plan",
    )(eid, su, lt)


def _tile_dma(src_hbm, row, dst, r, sem):
    dst_rows = r * SUBLANES if isinstance(r, int) else pl.multiple_of(r * SUBLANES, SUBLANES)
    return pltpu.make_async_copy(
        src_hbm.at[pl.ds(pl.multiple_of(row * SUBLANES, SUBLANES), SUBLANES)],
        dst.at[pl.ds(dst_rows, SUBLANES)], sem)


def _gather_tiles(n, row_of, src_hbm, dsts, sem, inline):
    def start(r, j):
        for k, dst in enumerate(dsts):
            _tile_dma(src_hbm, row_of(k, r), dst, r, sem).start(priority=(j + k) % 2)

    if inline:
        for r in range(n):
            start(r, r)
    else:
        def block(blk, carry):
            for j in range(ISSUE_UNROLL):
                start(blk * ISSUE_UNROLL + j, j)
            return carry
        lax.fori_loop(0, n // ISSUE_UNROLL, block, 0)


def _expert_kernel(te_ref, pos_ref, pad_ref, x_hbm, fg_ref, wg_ref, wu_ref, wd_ref, ys_ref,
                   *scratch, n_tiles, n_tokens):
    bufs, sems, inv = scratch[:N_XBUF], scratch[N_XBUF], scratch[N_XBUF + 1]
    i = pl.program_id(0)

    @pl.when(i == 0)
    def _():
        def clear_expert(e, carry):
            def clear(p, c):
                inv[p] = 0
                return c
            return lax.fori_loop(pad_ref[e], pad_ref[LANES + e], clear, carry)
        lax.fori_loop(0, N_EXPERTS, clear_expert, 0)

        def scatter(t, carry):
            inv[pos_ref[t]] = t
            inv[pos_ref[n_tokens + t]] = t
            return carry
        lax.fori_loop(0, n_tokens, scatter, 0, unroll=8)

    live = te_ref[i] < N_EXPERTS
    pending = te_ref[jnp.maximum(i - GATHER_AHEAD, 0)] < N_EXPERTS
    can_start = i + GATHER_AHEAD < n_tiles

    def gather(tile, p, inline):
        _gather_tiles(ROW_TILE, lambda k, r: inv[tile * ROW_TILE + r], x_hbm, [bufs[p]],
                      sems.at[p], inline)

    def gather_wait(p):
        pltpu.make_async_copy(x_hbm.at[pl.ds(0, ROW_TILE * SUBLANES)], bufs[p], sems.at[p]).wait()

    def compute(p):
        x = _rows_from_tiles(bufs[p], ROW_TILE)
        h = _rms_rows(x, fg_ref[...]).astype(BF16)
        hid = _silu(_dot(h, wg_ref[0].astype(BF16))) * _dot(h, wu_ref[0].astype(BF16))
        _rows_to_tiles(ys_ref, _dot(hid.astype(BF16), wd_ref[0].astype(BF16)))

    @pl.when(jnp.logical_and(i == 0, live))
    def _():
        for tile in range(min(GATHER_AHEAD, n_tiles)):
            gather(tile, tile % N_XBUF, False)

    def residue_step(p):
        mine = (i % N_XBUF) == p

        @pl.when(mine & live & can_start)
        def _():
            gather_wait(p)
            gather(i + GATHER_AHEAD, (p + GATHER_AHEAD) % N_XBUF, True)
            compute(p)

        if any((n_tiles - back) % N_XBUF == p for back in range(1, GATHER_AHEAD + 1)):
            @pl.when(mine & live & jnp.logical_not(can_start))
            def _():
                gather_wait(p)
                compute(p)

        @pl.when(mine & jnp.logical_not(live) & pending)
        def _():
            gather_wait(p)

    for p in range(N_XBUF):
        residue_step(p)

    @pl.when(jnp.logical_not(live))
    def _():
        ys_ref[...] = jnp.zeros(ys_ref.shape, F32)


def _experts(tile_expert, pos_flat, pad_flat, x_tiles, fg, wg, wu, wd, n_rows, layer):
    d = fg.shape[1]
    expert = lambda i, te, pos, pad: (layer, jnp.minimum(te[i], N_EXPERTS - 1), 0, 0)
    grid_spec = pltpu.PrefetchScalarGridSpec(
        num_scalar_prefetch=3,
        grid=(n_rows // ROW_TILE,),
        in_specs=[
            pl.BlockSpec(memory_space=pl.ANY),
            pl.BlockSpec((1, d), lambda i, te, pos, pad: (0, 0)),
            pl.BlockSpec((None, 1, d, D_EXPERT), expert),
            pl.BlockSpec((None, 1, d, D_EXPERT), expert),
            pl.BlockSpec((None, 1, D_EXPERT, d), expert),
        ],
        out_specs=pl.BlockSpec((ROW_TILE * SUBLANES, LANES), lambda i, te, pos, pad: (i, 0)),
        scratch_shapes=[pltpu.VMEM((ROW_TILE * SUBLANES, LANES), F32)] * N_XBUF
        + [pltpu.SemaphoreType.DMA((N_XBUF,)), pltpu.SMEM((n_rows,), I32)],
    )
    return pl.pallas_call(
        functools.partial(_expert_kernel, n_tiles=n_rows // ROW_TILE,
                          n_tokens=pos_flat.shape[0] // 2),
        grid_spec=grid_spec,
        out_shape=jax.ShapeDtypeStruct((n_rows * SUBLANES, LANES), F32),
        compiler_params=pltpu.CompilerParams(
            dimension_semantics=("arbitrary",), vmem_limit_bytes=VMEM_LIMIT_BYTES),
        name="experts",
    )(tile_expert, pos_flat, pad_flat, x_tiles, fg, wg, wu, wd)


def _combine_kernel(pos_ref, x_ref, w_ref, fin_ref, ys_hbm, out_ref, ybuf0, ybuf1, sems,
                    *, n_tokens, tc, n_steps, final_norm):
    i = pl.program_id(0)
    bufs = (ybuf0, ybuf1)

    def gather(step, p, inline):
        _gather_tiles(tc, lambda slot, r: pos_ref[slot * n_tokens + step * tc + r], ys_hbm,
                      [bufs[p].at[0], bufs[p].at[1]], sems.at[p], inline)

    def gather_wait(p):
        for slot in range(2):
            pltpu.make_async_copy(ys_hbm.at[pl.ds(0, tc * SUBLANES)], bufs[p].at[slot],
                                  sems.at[p]).wait()

    def compute(p):
        w = w_ref[...]
        out = (_rows_from_tiles(x_ref, tc) + w[:, 0:1] * _rows_from_tiles(bufs[p].at[0], tc)
               + w[:, 1:2] * _rows_from_tiles(bufs[p].at[1], tc))
        if final_norm:
            out = _rms_rows(out, fin_ref[...])
        out_ref[...] = out

    @pl.when(i == 0)
    def _():
        gather(0, 0, False)

    def parity_step(p):
        mine = (i % 2) == p

        @pl.when(mine & (i + 1 < n_steps))
        def _():
            gather_wait(p)
            gather(i + 1, 1 - p, True)
            compute(p)

        if (n_steps - 1) % 2 == p:
            @pl.when(mine & (i + 1 == n_steps))
            def _():
                gather_wait(p)
                compute(p)

    parity_step(0)
    parity_step(1)


def _combine(pos_flat, x_tiles, wts, fin_gain, ys_tiles, tc, final_norm):
    n = wts.shape[0]
    d = fin_gain.shape[1]
    grid_spec = pltpu.PrefetchScalarGridSpec(
        num_scalar_prefetch=1,
        grid=(n // tc,),
        in_specs=[
            pl.BlockSpec((tc * SUBLANES, LANES), lambda i, pos: (i, 0)),
            pl.BlockSpec((tc, SUBLANES), lambda i, pos: (i, 0)),
            pl.BlockSpec((1, d), lambda i, pos: (0, 0)),
            pl.BlockSpec(memory_space=pl.ANY),
        ],
        out_specs=pl.BlockSpec((tc, d), lambda i, pos: (i, 0)),
        scratch_shapes=[pltpu.VMEM((2, tc * SUBLANES, LANES), F32),
                        pltpu.VMEM((2, tc * SUBLANES, LANES), F32),
                        pltpu.SemaphoreType.DMA((2,))],
    )
    return pl.pallas_call(
        functools.partial(_combine_kernel, n_tokens=n, tc=tc, n_steps=n // tc,
                          final_norm=final_norm),
        grid_spec=grid_spec,
        out_shape=jax.ShapeDtypeStruct((n, d), F32),
        compiler_params=pltpu.CompilerParams(
            dimension_semantics=("arbitrary",), vmem_limit_bytes=VMEM_LIMIT_BYTES),
        name="combine",
    )(pos_flat, x_tiles, wts, fin_gain, ys_tiles)


def _tile(n, target):
    t = min(n, target)
    while n % t:
        t //= 2
    return t


def _widen_w_in(w):
    o = 0
    conv = w[:, o:o + 2 * CONV_WIDTH]; o += 2 * CONV_WIDTH
    qkv = w[:, o:o + QKV_DIM]; o += QKV_DIM
    zg = w[:, o:o + VAL_DIM]; o += VAL_DIM
    wb = w[:, o:o + N_HEADS]; o += N_HEADS
    wa = w[:, o:o + N_HEADS]; o += N_HEADS
    gc = w[:, o:o + D_MODEL]; o += D_MODEL
    gd = w[:, o:o + D_MODEL]
    return jnp.concatenate(
        [conv, qkv, zg, jnp.repeat(wb, HEAD_K, axis=1), jnp.repeat(wa, HEAD_K, axis=1), gc, gd],
        axis=1).astype(BF16)


def kernel(x, norm_mix_gain, w_in, conv_dw_w, conv_dw_b, conv_ln_gain, conv_ln_bias, w_conv_out,
           gdn_conv_w, gdn_a_log, gdn_dt_bias, gdn_norm_gain, w_gdn_out, w_out,
           norm_ffn_gain, w_router_group, b_router_group, w_router_expert, b_router_expert,
           w_expert_gate, w_expert_up, w_expert_down, final_norm_gain):
    b, t, d = x.shape
    n = b * t
    depth = w_in.shape[0]
    tm = _tile(t, 512)
    tc_gdn = _tile(t, CHUNK)
    tm_merge = _tile(n, 512)
    tp = _tile(n, 1024)
    tc_comb = _tile(n, 512)
    n_rows = -(-(2 * n + N_EXPERTS * (ROW_TILE - 1)) // ROW_TILE) * ROW_TILE
    te_pad = -(-(n_rows // ROW_TILE) // LANES) * LANES

    head_of_lane = jnp.arange(KEY_DIM) // HEAD_K
    ones_bd = (head_of_lane[:, None] == head_of_lane[None, :]).astype(BF16)
    bdm = ones_bd[:HALF, :HALF]
    eye = jnp.eye(tm_merge, dtype=BF16)
    su = (jnp.arange(tp)[:, None] < jnp.arange(tp)[None, :]).astype(BF16)
    lt = (jnp.arange(N_EXPERTS)[:, None] > jnp.arange(N_EXPERTS)[None, :]).astype(BF16)
    row = lambda a: a.reshape(1, -1).astype(F32)

    for l in range(depth):
        yc, q, k, v, beta_x, g_x, szg, sd = _inproj(
            x, row(norm_mix_gain[l]), _widen_w_in(w_in[l]), conv_dw_w[l], row(conv_dw_b[l]),
            row(conv_ln_gain[l]), row(conv_ln_bias[l]), w_conv_out[l].astype(BF16), gdn_conv_w[l],
            row(jnp.repeat(gdn_a_log[l], HEAD_K)), row(jnp.repeat(gdn_dt_bias[l], HEAD_K)),
            ones_bd, tm)
        o = _gdn(q, k, v, beta_x, g_x, bdm, tc_gdn)

        wr_t = jnp.concatenate(
            [w_router_expert[l].T, w_router_group[l].T,
             jnp.zeros((ROUTER_ROWS - N_EXPERTS - N_GROUPS, d), F32)], axis=0).astype(BF16)
        br_t = jnp.concatenate(
            [b_router_expert[l], b_router_group[l],
             jnp.zeros((ROUTER_ROWS - N_EXPERTS - N_GROUPS,), F32)]).reshape(ROUTER_ROWS, 1)
        x_mid, eid, wts = _merge(
            x.reshape(n, d), yc.reshape(n, d), o.reshape(n, VAL_DIM), szg.reshape(n, VAL_DIM),
            sd.reshape(n, d), row(jnp.tile(gdn_norm_gain[l], N_HEADS)), w_gdn_out[l].astype(BF16),
            w_out[l].astype(BF16), row(norm_ffn_gain[l]), wr_t, br_t, ones_bd, eye, tm_merge)

        pos, tile_expert, pad_rows = _plan(eid, su, lt, tp, te_pad, n_rows)
        pos_flat = pos.reshape(2 * n)
        ys = _experts(tile_expert.reshape(te_pad), pos_flat, pad_rows.reshape(2 * LANES), x_mid,
                      row(norm_ffn_gain[l]), w_expert_gate, w_expert_up, w_expert_down, n_rows, l)
        last = l == depth - 1
        x = _combine(pos_flat, x_mid, wts, row(final_norm_gain), ys, tc_comb, last
                     ).reshape(b, t, d)
    return x
```
